```python
import math
import jax, jax.numpy as jnp
from jax import lax
import numpy as np

D_MODEL = 1024
BATCH = 4
SEQ = 8192
DEPTH = 2
DEC_BATCH = 8
DEC_SEQ = 32
PAST_LEN = 2048

CHUNK = 64
N_MEM = 256
Q_BLOCK = 128
EPS = 1e-6
N_EVEN = (DEPTH + 1) // 2
N_ODD = DEPTH // 2
H_A = 4
DK_A = 128
DV_A = 128
W_A = H_A * DV_A
H_B = 4
DK_B = 64
DV_B = 2 * DK_B
W_B = H_B * DV_B
H_X = 4
DH_X = 128
W_X = H_X * DH_X
W_C = D_MODEL
CONV_W = 31
ALIBI_SLOPES = tuple(2.0 ** (-8.0 * (h + 1) / H_B) for h in range(H_B))
SPLIT_EVEN = (H_A * DK_A, H_A * DK_A, W_A, W_A, W_A, H_A, H_A, 2 * H_B * DK_B, 2 * H_B * DK_B, W_B, W_B, W_X, W_X)
N_IN_EVEN = sum(SPLIT_EVEN)
SPLIT_ODD = (W_C, W_C, W_C, W_X, W_X)
N_IN_ODD = sum(SPLIT_ODD)

kernel_name = 'hybrid_mlstm_diffattn_conformer_stream_step'


def lambda_init(layer):
    return 0.8 - 0.6 * math.exp(-0.3 * layer)


def rmsnorm(x, g):
    xf = x.astype(jnp.float32)
    y = xf * lax.rsqrt(jnp.mean(xf * xf, axis=-1, keepdims=True) + EPS)
    return (y * g.astype(jnp.float32)).astype(x.dtype)


def layernorm(x, g, b):
    xf = x.astype(jnp.float32)
    mu = jnp.mean(xf, axis=-1, keepdims=True)
    var = jnp.mean(jnp.square(xf - mu), axis=-1, keepdims=True)
    y = (xf - mu) * lax.rsqrt(var + EPS) * g.astype(jnp.float32) + b.astype(jnp.float32)
    return y.astype(x.dtype)


def split_cols(y, sizes):
    idx = [int(i) for i in np.cumsum(sizes)[:-1]]
    return jnp.split(y, idx, axis=-1)


def mlstm_chunk(state, inp):
    C0, n0, m0 = state
    q, k, v, ig, lf = inp
    L = q.shape[1]
    b = jnp.cumsum(lf, axis=1).transpose(0, 2, 1)
    it = ig.transpose(0, 2, 1)
    causal = jnp.tril(jnp.ones((L, L), dtype=bool))
    logw = jnp.where(causal, b[..., :, None] - b[..., None, :] + it[..., None, :], -jnp.inf)
    g = b + m0[..., None]
    m = jnp.maximum(g, jnp.max(logw, axis=-1))
    w_intra = jnp.exp(logw - m[..., None])
    w_inter = jnp.exp(g - m)
    s = jnp.einsum('blhd,bshd->bhls', q, k) * w_intra
    num = w_inter[..., None] * jnp.einsum('blhd,bhde->bhle', q, C0) + jnp.einsum('bhls,bshe->bhle', s, v)
    den = w_inter * jnp.einsum('blhd,bhd->bhl', q, n0) + jnp.sum(s, axis=-1)
    h = num / jnp.maximum(jnp.abs(den), jnp.exp(-m))[..., None]
    m_last = m[..., -1]
    decay = jnp.exp(g[..., -1] - m_last)
    w_end = jnp.exp(b[..., -1:] - b + it - m_last[..., None])
    C1 = decay[..., None, None] * C0 + jnp.einsum('bhs,bshd,bshe->bhde', w_end, k, v)
    n1 = decay[..., None] * n0 + jnp.einsum('bhs,bshd->bhd', w_end, k)
    return (C1, n1, m_last), h.transpose(0, 2, 1, 3)


def diff_attn_core(q, k, v, qpos, kpos, lam):
    s = jnp.einsum('bqhcd,bkhcd->bhcqk', q, k).astype(jnp.float32) * (DK_B ** -0.5)
    slopes = jnp.array(ALIBI_SLOPES, dtype=jnp.float32)
    dist = jnp.abs(qpos[:, None] - kpos[None, :]).astype(jnp.float32)
    s = s - slopes[:, None, None, None] * dist
    visible = (kpos // CHUNK)[None, :] <= (qpos // CHUNK)[:, None]
    s = jnp.where(visible, s, -jnp.inf)
    p = jax.nn.softmax(s, axis=-1)
    w = p[:, :, 0] - lam * p[:, :, 1]
    return jnp.einsum('bhqk,bkhd->bqhd', w.astype(v.dtype), v)


def diff_attn_prompt(q, k, v, lam):
    bsz, S = q.shape[0], q.shape[1]
    nb = S // Q_BLOCK
    qb = q.reshape(bsz, nb, Q_BLOCK, H_B, 2, DK_B).swapaxes(0, 1)
    kpos = jnp.arange(S)

    def block(args):
        qblk, j = args
        qpos = j * Q_BLOCK + jnp.arange(Q_BLOCK)
        return diff_attn_core(qblk, k, v, qpos, kpos, lam)

    o = lax.map(block, (qb, jnp.arange(nb)))
    return o.swapaxes(0, 1).reshape(bsz, S, H_B, DV_B)


def memory_kv(mem, g, w_kv, kg):
    bsz, M = mem.shape[0], mem.shape[1]
    kv = rmsnorm(mem, g) @ w_kv
    k, v = jnp.split(kv, 2, axis=-1)
    k = rmsnorm(k.reshape(bsz, M, H_X, DH_X), kg)
    return k, v.reshape(bsz, M, H_X, DH_X)


def cross_attn(q, mem_k, mem_v):
    bsz, L = q.shape[0], q.shape[1]
    s = jnp.einsum('blhd,bmhd->bhlm', q, mem_k.astype(q.dtype)).astype(jnp.float32) * (DH_X ** -0.5)
    p = jax.nn.softmax(s, axis=-1).astype(mem_v.dtype)
    return jnp.einsum('bhlm,bmhd->blhd', p, mem_v).reshape(bsz, L, W_X)


def even_layer(x, mem_k, mem_v, hist, norm_g, w_in, b_ig, b_fg, mlstm_g, qn_g, kn_g,
               lq1, lk1, lq2, lk2, subln_g, w_out, xq_g, lam_init):
    f32 = jnp.float32
    bsz, L = x.shape[0], x.shape[1]
    y = rmsnorm(x, norm_g) @ w_in
    aq, ak, av, ao, az, ai, af, bq, bk, bv, bz, xq, xz = split_cols(y, SPLIT_EVEN)
    qa = aq.reshape(bsz, L, H_A, DK_A).astype(f32)
    ka = ak.reshape(bsz, L, H_A, DK_A).astype(f32) * (DK_A ** -0.5)
    va = av.reshape(bsz, L, H_A, DV_A).astype(f32)
    ig = ai.astype(f32) + b_ig.astype(f32)
    lf = jax.nn.log_sigmoid(af.astype(f32) + b_fg.astype(f32))
    qb = rmsnorm(bq.reshape(bsz, L, H_B, 2, DK_B), qn_g)
    kb = rmsnorm(bk.reshape(bsz, L, H_B, 2, DK_B), kn_g)
    vb = bv.reshape(bsz, L, H_B, DV_B)
    lam = (jnp.exp(jnp.sum(lq1.astype(f32) * lk1.astype(f32)))
           - jnp.exp(jnp.sum(lq2.astype(f32) * lk2.astype(f32))) + lam_init)
    if hist is None:
        state0 = (jnp.zeros((bsz, H_A, DK_A, DV_A), f32), jnp.zeros((bsz, H_A, DK_A), f32),
                  jnp.zeros((bsz, H_A), f32))
        n_chunks = L // CHUNK

        def to_chunks(a):
            return a.reshape(bsz, n_chunks, CHUNK, *a.shape[2:]).swapaxes(0, 1)

        state1, ha = lax.scan(mlstm_chunk, state0, tuple(to_chunks(a) for a in (qa, ka, va, ig, lf)))
        ha = ha.swapaxes(0, 1).reshape(bsz, L, H_A, DV_A)
        ob = diff_attn_prompt(qb, kb, vb, lam)
    else:
        k_past, v_past, C0, n0, m0 = hist
        state1, ha = mlstm_chunk((C0.astype(f32), n0.astype(f32), m0.astype(f32)), (qa, ka, va, ig, lf))
        P = k_past.shape[1]
        k_all = jnp.concatenate([k_past.reshape(bsz, P, H_B, 2, DK_B).astype(kb.dtype), kb], axis=1)
        v_all = jnp.concatenate([v_past.astype(vb.dtype), vb], axis=1)
        ob = diff_attn_core(qb, k_all, v_all, P + jnp.arange(L), jnp.arange(P + L), lam)
    oa = rmsnorm(ha, mlstm_g) * jax.nn.sigmoid(ao.astype(f32)).reshape(bsz, L, H_A, DV_A)
    ob = rmsnorm(ob, subln_g) * (1.0 - lam_init)
    ox = cross_attn(rmsnorm(xq.reshape(bsz, L, H_X, DH_X), xq_g), mem_k, mem_v)
    mixed = jnp.concatenate([
        oa.reshape(bsz, L, W_A).astype(x.dtype) * jax.nn.silu(az),
        ob.reshape(bsz, L, W_B) * jax.nn.silu(bz),
        ox.astype(x.dtype) * jax.nn.silu(xz)], axis=-1)
    C1, n1, m1 = state1
    return x + mixed @ w_out, kb.reshape(bsz, L, H_B, 2 * DK_B), vb, C1, n1, m1


def odd_layer(x, mem_k, mem_v, conv_hist, norm_g, w_in, conv_w, conv_b, ln_g, ln_b, w_out, xq_g):
    bsz, L = x.shape[0], x.shape[1]
    y = rmsnorm(x, norm_g) @ w_in
    cu, cg, cz, xq, xz = split_cols(y, SPLIT_ODD)
    u = cu * jax.nn.sigmoid(cg)
    if conv_hist is None:
        u_pad = jnp.pad(u, ((0, 0), (CONV_W - 1, 0), (0, 0)))
    else:
        u_pad = jnp.concatenate([conv_hist.astype(u.dtype), u], axis=1)
    c = lax.conv_general_dilated(u_pad, conv_w.astype(u.dtype)[:, None, :], (1,), 'VALID',
                                 dimension_numbers=('NWC', 'WIO', 'NWC'),
                                 feature_group_count=W_C) + conv_b
    c = jax.nn.silu(layernorm(c, ln_g, ln_b))
    ox = cross_attn(rmsnorm(xq.reshape(bsz, L, H_X, DH_X), xq_g), mem_k, mem_v)
    mixed = jnp.concatenate([c * jax.nn.silu(cz), ox.astype(x.dtype) * jax.nn.silu(xz)], axis=-1)
    return x + mixed @ w_out, u_pad[:, -(CONV_W - 1):]


def setup_inputs(seed: int = 0) -> dict:
    key = jax.random.key(seed)
    keys = iter(jax.random.split(key, 48))

    def nrm(shape, scale):
        return jax.random.normal(next(keys), shape, jnp.float32) * scale

    def gain(shape):
        return 1.0 + nrm(shape, 0.02)

    d_in_a = W_A + W_B + W_X
    d_in_c = W_C + W_X
    return {
        'x_prompt': nrm((BATCH, SEQ, D_MODEL), 1.0),
        'x_sample': nrm((DEC_BATCH, DEC_SEQ, D_MODEL), 1.0),
        'mem_prompt': nrm((BATCH, N_MEM, D_MODEL), 1.0),
        'cache_xk': nrm((DEPTH, DEC_BATCH, N_MEM, H_X, DH_X), 1.0),
        'cache_xv': nrm((DEPTH, DEC_BATCH, N_MEM, H_X, DH_X), 1.0),
        'cache_k': nrm((N_EVEN, DEC_BATCH, PAST_LEN, H_B, 2 * DK_B), 1.0),
        'cache_v': nrm((N_EVEN, DEC_BATCH, PAST_LEN, H_B, DV_B), 1.0),
        'state_C': nrm((N_EVEN, DEC_BATCH, H_A, DK_A, DV_A), 0.5),
        'state_n': nrm((N_EVEN, DEC_BATCH, H_A, DK_A), 0.3),
        'state_m': nrm((N_EVEN, DEC_BATCH, H_A), 0.5),
        'state_conv': nrm((N_ODD, DEC_BATCH, CONV_W - 1, W_C), 0.5),
        'norm_g': gain((DEPTH, D_MODEL)),
        'w_in_a': nrm((N_EVEN, D_MODEL, N_IN_EVEN), D_MODEL ** -0.5),
        'b_ig': nrm((N_EVEN, H_A), 0.1),
        'b_fg': jnp.linspace(3.0, 6.0, H_A, dtype=jnp.float32)[None, :] + nrm((N_EVEN, H_A), 0.1),
        'mlstm_norm_g': gain((N_EVEN, H_A, DV_A)),
        'qn_g': gain((N_EVEN, DK_B)),
        'kn_g': gain((N_EVEN, DK_B)),
        'lam_q1': nrm((N_EVEN, DK_B), 0.1),
        'lam_k1': nrm((N_EVEN, DK_B), 0.1),
        'lam_q2': nrm((N_EVEN, DK_B), 0.1),
        'lam_k2': nrm((N_EVEN, DK_B), 0.1),
        'subln_g': gain((N_EVEN, DV_B)),
        'w_out_a': nrm((N_EVEN, d_in_a, D_MODEL), d_in_a ** -0.5),
        'w_in_c': nrm((N_ODD, D_MODEL, N_IN_ODD), D_MODEL ** -0.5),
        'conv_w': nrm((N_ODD, CONV_W, W_C), CONV_W ** -0.5),
        'conv_b': nrm((N_ODD, W_C), 0.02),
        'conv_ln_g': gain((N_ODD, W_C)),
        'conv_ln_b': nrm((N_ODD, W_C), 0.02),
        'w_out_c': nrm((N_ODD, d_in_c, D_MODEL), d_in_c ** -0.5),
        'mem_norm_g': gain((DEPTH, D_MODEL)),
        'w_mem_kv': nrm((DEPTH, D_MODEL, 2 * W_X), D_MODEL ** -0.5),
        'xq_norm_g': gain((DEPTH, DH_X)),
        'xk_norm_g': gain((DEPTH, DH_X)),
    }


def reference(x_prompt, x_sample, mem_prompt, cache_xk, cache_xv, cache_k, cache_v, state_C, state_n,
              state_m, state_conv, norm_g, w_in_a, b_ig, b_fg, mlstm_norm_g, qn_g, kn_g, lam_q1, lam_k1,
              lam_q2, lam_k2, subln_g, w_out_a, w_in_c, conv_w, conv_b, conv_ln_g, conv_ln_b, w_out_c,
              mem_norm_g, w_mem_kv, xq_norm_g, xk_norm_g):
    yp, ys = x_prompt, x_sample
    p_xk, p_xv, p_k, p_v, p_C, p_n, p_m, p_conv = [], [], [], [], [], [], [], []
    s_k, s_v, s_C, s_n, s_m, s_conv = [], [], [], [], [], []
    for layer in range(DEPTH):
        mk, mv = memory_kv(mem_prompt, mem_norm_g[layer], w_mem_kv[layer], xk_norm_g[layer])
        p_xk.append(mk)
        p_xv.append(mv)
        if layer % 2 == 0:
            e = layer // 2
            wts = (norm_g[layer], w_in_a[e], b_ig[e], b_fg[e], mlstm_norm_g[e], qn_g[e], kn_g[e],
                   lam_q1[e], lam_k1[e], lam_q2[e], lam_k2[e], subln_g[e], w_out_a[e], xq_norm_g[layer],
                   lambda_init(layer))
            yp, k_new, v_new, C1, n1, m1 = even_layer(yp, mk, mv, None, *wts)
            p_k.append(k_new)
            p_v.append(v_new)
            p_C.append(C1)
            p_n.append(n1)
            p_m.append(m1)
            hist = (cache_k[e], cache_v[e], state_C[e], state_n[e], state_m[e])
            ys, k_new, v_new, C1, n1, m1 = even_layer(ys, cache_xk[layer], cache_xv[layer], hist, *wts)
            s_k.append(k_new)
            s_v.append(v_new)
            s_C.append(C1)
            s_n.append(n1)
            s_m.append(m1)
        else:
            o = layer // 2
            wts = (norm_g[layer], w_in_c[o], conv_w[o], conv_b[o], conv_ln_g[o], conv_ln_b[o], w_out_c[o],
                   xq_norm_g[layer])
            yp, cv = odd_layer(yp, mk, mv, None, *wts)
            p_conv.append(cv)
            ys, cv = odd_layer(ys, cache_xk[layer], cache_xv[layer], state_conv[o], *wts)
            s_conv.append(cv)
    new_p_xk = jnp.stack(p_xk)
    new_p_xv = jnp.stack(p_xv)
    new_p_k = jnp.stack(p_k)
    new_p_v = jnp.stack(p_v)
    new_p_C = jnp.stack(p_C)
    new_p_n = jnp.stack(p_n)
    new_p_m = jnp.stack(p_m)
    new_p_conv = jnp.stack(p_conv)
    new_s_k = jnp.stack(s_k)
    new_s_v = jnp.stack(s_v)
    new_s_C = jnp.stack(s_C)
    new_s_n = jnp.stack(s_n)
    new_s_m = jnp.stack(s_m)
    new_s_conv = jnp.stack(s_conv)
    return (yp, ys, new_p_xk, new_p_xv, new_p_k, new_p_v, new_p_C, new_p_n, new_p_m, new_p_conv,
            new_s_k, new_s_v, new_s_C, new_s_n, new_s_m, new_s_conv)
```

```python
import functools
import math

import jax
import jax.numpy as jnp
from jax import lax
from jax.experimental import pallas as pl
from jax.experimental.pallas import tpu as pltpu

F32 = jnp.float32
BF16 = jnp.bfloat16

D_MODEL = 1024
CHUNK = 64
EPS = 1e-6
N_HEADS = 4
HEAD_W = 128
MIX_W = N_HEADS * HEAD_W
DK_B = 64
CONV_W = 31
CONV_HIST = 32
N_MEM = 256
ALIBI_SLOPES = tuple(2.0 ** (-8.0 * (h + 1) / N_HEADS) for h in range(N_HEADS))
VMEM_LIMIT = 56 * 1024 * 1024

NEG_INF = float("-inf")


def _lambda_init(layer):
    return 0.8 - 0.6 * math.exp(-0.3 * layer)


def _params(sem):
    return pltpu.CompilerParams(dimension_semantics=sem, vmem_limit_bytes=VMEM_LIMIT)


def _const_spec(shape):
    zeros = (0,) * len(shape)
    return pl.BlockSpec(shape, lambda *_: zeros)


def _rms_rows(x, g):
    return x * lax.rsqrt(jnp.mean(x * x, axis=-1, keepdims=True) + EPS) * g


def _sigmoid(x):
    return 1.0 / (1.0 + jnp.exp(-x))


def _silu(x):
    return x * _sigmoid(x)


def _dot(a, b):
    return jnp.dot(a, b, preferred_element_type=F32)


def _dot_nt(a, b):
    return lax.dot_general(a, b, (((1,), (1,)), ((), ())), preferred_element_type=F32)


def _dot_tn(a, b):
    return lax.dot_general(a, b, (((0,), (0,)), ((), ())), preferred_element_type=F32)


def _ones_col(rows):
    lane = lax.broadcasted_iota(jnp.int32, (rows, HEAD_W), 1)
    return jnp.where(lane == 0, 1.0, 0.0).astype(BF16)


def _mem_kv_kernel(mem_ref, g_ref, w_ref, kg_ref, k_ref, v_ref):
    xn = _rms_rows(mem_ref[0], g_ref[0]).astype(BF16)
    kv = _dot(xn, w_ref[0])
    kg = kg_ref[0]
    for h in range(N_HEADS):
        sl = slice(h * HEAD_W, (h + 1) * HEAD_W)
        k_ref[0, 0, :, sl] = _rms_rows(kv[:, sl], kg)
    v_ref[0, 0] = kv[:, MIX_W:]


def _mem_kv(mem, g, w_kv, kg):
    depth = w_kv.shape[0]
    bsz, n_mem, _ = mem.shape
    out = jax.ShapeDtypeStruct((depth, bsz, n_mem, MIX_W), F32)
    return pl.pallas_call(
        _mem_kv_kernel,
        grid=(depth, bsz),
        in_specs=[
            pl.BlockSpec((1, n_mem, D_MODEL), lambda l, b: (b, 0, 0)),
            pl.BlockSpec((1, 1, D_MODEL), lambda l, b: (l, 0, 0)),
            pl.BlockSpec((1, D_MODEL, 2 * MIX_W), lambda l, b: (l, 0, 0)),
            pl.BlockSpec((1, 1, HEAD_W), lambda l, b: (l, 0, 0)),
        ],
        out_specs=[pl.BlockSpec((1, 1, n_mem, MIX_W), lambda l, b: (l, b, 0, 0))] * 2,
        out_shape=[out, out],
        compiler_params=_params(("arbitrary", "arbitrary")),
        name="mem_kv",
    )(mem, g.reshape(depth, 1, D_MODEL), w_kv.astype(BF16), kg.reshape(depth, 1, HEAD_W))


_E_AQ, _E_AK, _E_AV, _E_AO, _E_AZ, _E_BQ, _E_BK, _E_BV, _E_BZ, _E_XQ, _E_XZ = (i * MIX_W for i in range(11))
_E_MAIN = 11 * MIX_W


def _group_mean_sq(y, grp):
    sq = y * y
    hi = sq.astype(BF16)
    lo = (sq - hi.astype(F32)).astype(BF16)
    return (_dot(hi, grp) + _dot(lo, grp)) * (1.0 / DK_B)


def _in_even_kernel(x_ref, g_ref, w_ref, wg_ref, gbias_ref, grp_ref, qn_ref, kn_ref, xqg_ref,
                    qa_ref, ka_ref, va_ref, ga_ref, gt_ref, qb_ref, kb_ref, kbh_ref, vb_ref, vbh_ref,
                    gb_ref, xq_ref, gx_ref):
    xn = _rms_rows(x_ref[...], g_ref[...]).astype(BF16)

    def proj(c0):
        return _dot(xn, w_ref[:, c0:c0 + MIX_W])

    qa_ref[...] = proj(_E_AQ).astype(BF16)
    ka_ref[...] = (proj(_E_AK) * (HEAD_W ** -0.5)).astype(BF16)
    va_ref[...] = proj(_E_AV).astype(BF16)
    ga_ref[...] = (_sigmoid(proj(_E_AO)) * _silu(proj(_E_AZ))).astype(BF16)
    gt = _dot(xn, wg_ref[...]) + gbias_ref[...]
    lane = lax.broadcasted_iota(jnp.int32, gt.shape, 1)
    log_sig = jnp.minimum(gt, 0.0) - jnp.log1p(jnp.exp(-jnp.abs(gt)))
    gt_ref[...] = jnp.where((lane >= N_HEADS) & (lane < 2 * N_HEADS), log_sig, gt)
    grp = grp_ref[...]
    bq = proj(_E_BQ)
    qb_ref[...] = (bq * lax.rsqrt(_group_mean_sq(bq, grp) + EPS) * qn_ref[...] * (DK_B ** -0.5)).astype(BF16)
    bk = proj(_E_BK)
    kb = bk * lax.rsqrt(_group_mean_sq(bk, grp) + EPS) * kn_ref[...]
    kb_ref[...] = kb
    kbh_ref[...] = kb.astype(BF16)
    bv = proj(_E_BV)
    vb_ref[...] = bv
    vbh_ref[...] = bv.astype(BF16)
    gb_ref[...] = _silu(proj(_E_BZ)).astype(BF16)
    xq = proj(_E_XQ)
    xqg = xqg_ref[...] * (HEAD_W ** -0.5)
    for h in range(N_HEADS):
        sl = slice(h * HEAD_W, (h + 1) * HEAD_W)
        xq_ref[:, sl] = _rms_rows(xq[:, sl], xqg).astype(BF16)
    gx_ref[...] = _silu(proj(_E_XZ)).astype(BF16)


def _in_even(x2d, norm_g, w_main, w_gate, gate_bias, grp, qn, kn, xqg, tm):
    m = x2d.shape[0]
    tm = min(tm, m)
    row = lambda width: pl.BlockSpec((tm, width), lambda i: (i, 0))
    half = jax.ShapeDtypeStruct((m, MIX_W), BF16)
    full = jax.ShapeDtypeStruct((m, MIX_W), F32)
    out_shape = [half, half, half, half, jax.ShapeDtypeStruct((m, HEAD_W), F32),
                 half, full, half, full, half, half, half, half]
    out_specs = [row(MIX_W)] * 4 + [row(HEAD_W)] + [row(MIX_W)] * 8
    return pl.pallas_call(
        _in_even_kernel,
        grid=(m // tm,),
        in_specs=[
            row(D_MODEL),
            _const_spec((1, D_MODEL)),
            _const_spec((D_MODEL, _E_MAIN)),
            _const_spec((D_MODEL, HEAD_W)),
            _const_spec((1, HEAD_W)),
            _const_spec((MIX_W, MIX_W)),
            _const_spec((1, MIX_W)),
            _const_spec((1, MIX_W)),
            _const_spec((1, HEAD_W)),
        ],
        out_specs=out_specs,
        out_shape=out_shape,
        compiler_params=_params(("arbitrary",)),
        name="in_even",
    )(x2d, norm_g, w_main, w_gate, gate_bias, grp, qn, kn, xqg)


def _mlstm_kernel(q_ref, k_ref, v_ref, gt_ref, ga_ref, ng_ref, s0_ref, m0_ref, o_ref, s_ref, m_ref, *, chunk):
    c = pl.program_id(1)

    @pl.when(c == 0)
    def _():
        s_ref[...] = s0_ref[...]
        m_ref[...] = m0_ref[...]

    gt = gt_ref[0]
    row = lax.broadcasted_iota(jnp.int32, (chunk, chunk), 0)
    col = lax.broadcasted_iota(jnp.int32, (chunk, chunk), 1)
    causal = col <= row
    cum = jnp.dot(causal.astype(F32), gt, preferred_element_type=F32,
                  precision=lax.Precision.HIGHEST)
    gt_t = gt.T
    cum_t = cum.T
    ones = _ones_col(chunk)
    for h in range(N_HEADS):
        sl = slice(h * HEAD_W, (h + 1) * HEAD_W)
        b_col = cum[:, N_HEADS + h:N_HEADS + h + 1]
        i_col = gt[:, h:h + 1]
        r_row = gt_t[h:h + 1, :] - cum_t[N_HEADS + h:N_HEADS + h + 1, :]
        m0 = m_ref[0, h][:, 0:1]
        g_col = b_col + m0
        logw = jnp.where(causal, b_col + r_row, NEG_INF)
        m_col = jnp.maximum(g_col, jnp.max(logw, axis=-1, keepdims=True))
        w_intra = jnp.exp(logw - m_col)
        w_inter = jnp.exp(g_col - m_col)
        q = q_ref[0, :, sl]
        k = k_ref[0, :, sl]
        v_aug = jnp.concatenate([v_ref[0, :, sl], ones], axis=-1)
        state = s_ref[0, h]
        sc = (_dot_nt(q, k) * w_intra).astype(BF16)
        inter = _dot(q, state.astype(BF16))
        intra = _dot(sc, v_aug)
        num = w_inter * inter[:, :HEAD_W] + intra[:, :HEAD_W]
        den = w_inter * inter[:, HEAD_W:HEAD_W + 1] + intra[:, HEAD_W:HEAD_W + 1]
        hid = num / jnp.maximum(jnp.abs(den), jnp.exp(-m_col))
        out = _rms_rows(hid, ng_ref[:, sl]) * ga_ref[0, :, sl].astype(F32)
        o_ref[0, :, sl] = out.astype(BF16)
        m_last = m_col[chunk - 1:chunk, :]
        decay = jnp.exp(g_col[chunk - 1:chunk, :] - m_last)
        w_end = jnp.exp(b_col[chunk - 1:chunk, :] - b_col + i_col - m_last)
        kv = _dot_tn(k, (v_aug.astype(F32) * w_end).astype(BF16))
        s_ref[0, h] = decay * state + kv
        m_ref[0, h] = jnp.broadcast_to(m_last, (1, HEAD_W))


def _mlstm(q, k, v, gates, gate_a, norm_g, state0, m0, chunk):
    bsz, seq, _ = q.shape
    chunk = min(chunk, seq)
    tok = lambda width: pl.BlockSpec((1, chunk, width), lambda b, c: (b, c, 0))
    st_spec = pl.BlockSpec((1, N_HEADS, HEAD_W, 2 * HEAD_W), lambda b, c: (b, 0, 0, 0))
    m_spec = pl.BlockSpec((1, N_HEADS, 1, HEAD_W), lambda b, c: (b, 0, 0, 0))
    return pl.pallas_call(
        functools.partial(_mlstm_kernel, chunk=chunk),
        grid=(bsz, seq // chunk),
        in_specs=[tok(MIX_W), tok(MIX_W), tok(MIX_W), tok(HEAD_W), tok(MIX_W),
                  _const_spec((1, MIX_W)), st_spec, m_spec],
        out_specs=[tok(MIX_W), st_spec, m_spec],
        out_shape=[jax.ShapeDtypeStruct((bsz, seq, MIX_W), BF16),
                   jax.ShapeDtypeStruct(state0.shape, F32),
                   jax.ShapeDtypeStruct(m0.shape, F32)],
        compiler_params=_params(("arbitrary", "arbitrary")),
        name="mlstm",
    )(q, k, v, gates, gate_a, norm_g, state0, m0)


def _split_maps(q):
    lane = lax.broadcasted_iota(jnp.int32, q.shape, 1)
    zero = jnp.zeros_like(q)
    return jnp.concatenate([jnp.where(lane < DK_B, q, zero), jnp.where(lane >= DK_B, q, zero)], axis=0)


def _diff_finish(acc, rows, lam_refs, lam_init, sg, gate):
    lq1, lk1, lq2, lk2 = (r[...] for r in lam_refs)
    lam = (jnp.exp(jnp.sum(lq1 * lk1, axis=-1, keepdims=True))
           - jnp.exp(jnp.sum(lq2 * lk2, axis=-1, keepdims=True)) + lam_init)
    o0 = acc[:rows, :HEAD_W] / acc[:rows, HEAD_W:HEAD_W + 1]
    o1 = acc[rows:, :HEAD_W] / acc[rows:, HEAD_W:HEAD_W + 1]
    o = o0 - lam * o1
    return (_rms_rows(o, sg) * (1.0 - lam_init) * gate.astype(F32)).astype(BF16)


def _diff_prompt_kernel(q_ref, k_ref, v_ref, gate_ref, slope_ref, lq1_ref, lk1_ref, lq2_ref, lk2_ref, sg_ref,
                        o_ref, m_sc, acc_sc, *, tq, lam_init):
    i = pl.program_id(2)
    slope = slope_ref[0][:, 0:1]
    qq = _split_maps(q_ref[0])
    m_sc[...] = jnp.full(m_sc.shape, NEG_INF, F32)
    acc_sc[...] = jnp.zeros(acc_sc.shape, F32)
    ones = _ones_col(tq)
    col_f = lax.broadcasted_iota(jnp.int32, (1, tq), 1).astype(F32)

    def block(j, score_bias):
        start = pl.multiple_of(j * tq, tq)
        k = k_ref[0, pl.ds(start, tq), :]
        v_aug = jnp.concatenate([v_ref[0, pl.ds(start, tq), :], ones], axis=-1)
        s = score_bias(_dot_nt(qq, k))
        m_old = m_sc[...]
        m_new = jnp.maximum(m_old, jnp.max(s, axis=-1, keepdims=True))
        p = jnp.exp(s - m_new)
        acc_sc[...] = jnp.exp(m_old - m_new) * acc_sc[...] + _dot(p.astype(BF16), v_aug)
        m_sc[...] = m_new

    def earlier(j, carry):
        block(j, lambda s: s + slope * (col_f + ((j - i) * tq).astype(F32)))
        return carry

    lax.fori_loop(0, i, earlier, 0)

    def diagonal(s):
        r = lax.broadcasted_iota(jnp.int32, (2 * tq, tq), 0)
        r = jnp.where(r >= tq, r - tq, r)
        c = lax.broadcasted_iota(jnp.int32, (2 * tq, tq), 1)
        visible = (c // CHUNK) <= (r // CHUNK)
        bias = slope * (r - jnp.abs(r - c)).astype(F32)
        return jnp.where(visible, s + bias, NEG_INF)

    block(i, diagonal)
    o_ref[0] = _diff_finish(acc_sc[...], tq, (lq1_ref, lk1_ref, lq2_ref, lk2_ref), lam_init,
                            sg_ref[...], gate_ref[0])


def _diff_prompt(q, k, v, gate, slopes, lam_vecs, subln_g, lam_init, tq):
    bsz, seq, _ = q.shape
    tq = min(tq, seq)
    q_spec = pl.BlockSpec((1, tq, HEAD_W), lambda b, h, i: (b, i, h))
    kv_spec = pl.BlockSpec((1, seq, HEAD_W), lambda b, h, i: (b, 0, h))
    return pl.pallas_call(
        functools.partial(_diff_prompt_kernel, tq=tq, lam_init=lam_init),
        grid=(bsz, N_HEADS, seq // tq),
        in_specs=[q_spec, kv_spec, kv_spec, q_spec,
                  pl.BlockSpec((1, 1, HEAD_W), lambda b, h, i: (h, 0, 0))]
                 + [_const_spec((1, DK_B))] * 4 + [_const_spec((1, HEAD_W))],
        out_specs=q_spec,
        out_shape=jax.ShapeDtypeStruct((bsz, seq, MIX_W), BF16),
        scratch_shapes=[pltpu.VMEM((2 * tq, 1), F32), pltpu.VMEM((2 * tq, 2 * HEAD_W), F32)],
        compiler_params=_params(("arbitrary", "arbitrary", "arbitrary")),
        name="diff_prompt",
    )(q, k, v, gate, slopes, *lam_vecs, subln_g)


def _diff_sample_kernel(q_ref, k_ref, v_ref, gate_ref, slope_ref, lq1_ref, lk1_ref, lq2_ref, lk2_ref, sg_ref,
                        o_ref, *, past, lam_init):
    rows = q_ref.shape[1]
    keys = k_ref.shape[1]
    slope = slope_ref[0][:, 0:1]
    qq = _split_maps(q_ref[0])
    r = lax.broadcasted_iota(jnp.int32, (2 * rows, keys), 0)
    r = jnp.where(r >= rows, r - rows, r) + past
    c = lax.broadcasted_iota(jnp.int32, (2 * rows, keys), 1)
    s = _dot_nt(qq, k_ref[0]) - slope * jnp.abs(r - c).astype(F32)
    p = jnp.exp(s - jnp.max(s, axis=-1, keepdims=True))
    v_aug = jnp.concatenate([v_ref[0], _ones_col(keys)], axis=-1)
    acc = _dot(p.astype(BF16), v_aug)
    o_ref[0] = _diff_finish(acc, rows, (lq1_ref, lk1_ref, lq2_ref, lk2_ref), lam_init, sg_ref[...], gate_ref[0])


def _diff_sample(q, k_all, v_all, gate, slopes, lam_vecs, subln_g, lam_init):
    bsz, rows, _ = q.shape
    keys = k_all.shape[1]
    q_spec = pl.BlockSpec((1, rows, HEAD_W), lambda b, h: (b, 0, h))
    kv_spec = pl.BlockSpec((1, keys, HEAD_W), lambda b, h: (b, 0, h))
    return pl.pallas_call(
        functools.partial(_diff_sample_kernel, past=keys - rows, lam_init=lam_init),
        grid=(bsz, N_HEADS),
        in_specs=[q_spec, kv_spec, kv_spec, q_spec,
                  pl.BlockSpec((1, 1, HEAD_W), lambda b, h: (h, 0, 0))]
                 + [_const_spec((1, DK_B))] * 4 + [_const_spec((1, HEAD_W))],
        out_specs=q_spec,
        out_shape=jax.ShapeDtypeStruct((bsz, rows, MIX_W), BF16),
        compiler_params=_params(("arbitrary", "arbitrary")),
        name="diff_sample",
    )(q, k_all, v_all, gate, slopes, *lam_vecs, subln_g)


def _cross_attn(xq, gate, mem_k, mem_v):
    outs = []
    for h in range(N_HEADS):
        sl = slice(h * HEAD_W, (h + 1) * HEAD_W)
        s = _dot_nt(xq[:, sl], mem_k[:, sl])
        p = jnp.exp(s - jnp.max(s, axis=-1, keepdims=True))
        p = p / jnp.sum(p, axis=-1, keepdims=True)
        outs.append(_dot(p.astype(BF16), mem_v[:, sl]))
    return (jnp.concatenate(outs, axis=-1) * gate).astype(BF16)


def _out_even_kernel(x_ref, oa_ref, ob_ref, xq_ref, gx_ref, mk_ref, mv_ref, w_ref, y_ref):
    ox = _cross_attn(xq_ref[0], gx_ref[0].astype(F32), mk_ref[0], mv_ref[0])
    mixed = jnp.concatenate([oa_ref[0], ob_ref[0], ox], axis=-1)
    y_ref[0] = x_ref[0] + _dot(mixed, w_ref[...])


def _out_even(x, oa, ob, xq, gx, mem_k, mem_v, w_out, tm):
    bsz, seq, _ = x.shape
    tm = min(tm, seq)
    tok = lambda width: pl.BlockSpec((1, tm, width), lambda b, i: (b, i, 0))
    mem_spec = pl.BlockSpec((1, N_MEM, MIX_W), lambda b, i: (b, 0, 0))
    return pl.pallas_call(
        _out_even_kernel,
        grid=(bsz, seq // tm),
        in_specs=[tok(D_MODEL), tok(MIX_W), tok(MIX_W), tok(MIX_W), tok(MIX_W), mem_spec, mem_spec,
                  _const_spec((3 * MIX_W, D_MODEL))],
        out_specs=tok(D_MODEL),
        out_shape=jax.ShapeDtypeStruct(x.shape, F32),
        compiler_params=_params(("arbitrary", "arbitrary")),
        name="out_even",
    )(x, oa, ob, xq, gx, mem_k, mem_v, w_out)


_O_CU, _O_CG, _O_CZ = 0, D_MODEL, 2 * D_MODEL
_O_XQ, _O_XZ = 3 * D_MODEL, 3 * D_MODEL + MIX_W


def _odd_kernel(x_ref, hist_ref, g_ref, w_in_ref, cw_ref, cb_ref, lg_ref, lb_ref, xqg_ref, mk_ref, mv_ref,
                w_out_ref, y_ref, tail_ref, u_sc, *, tm):
    i = pl.program_id(1)

    @pl.when(i == 0)
    def _():
        u_sc[0:CONV_HIST, :] = hist_ref[0]

    x = x_ref[0]
    xn = _rms_rows(x, g_ref[...]).astype(BF16)

    def proj(c0, width):
        return _dot(xn, w_in_ref[:, c0:c0 + width])

    u = proj(_O_CU, D_MODEL) * _sigmoid(proj(_O_CG, D_MODEL))
    u_sc[CONV_HIST:CONV_HIST + tm, :] = u
    acc = jnp.zeros((tm, D_MODEL), F32) + cb_ref[...]
    for j in range(CONV_W):
        off = CONV_HIST - (CONV_W - 1) + j
        acc = acc + cw_ref[j:j + 1, :] * u_sc[off:off + tm, :]
    mu = jnp.mean(acc, axis=-1, keepdims=True)
    cen = acc - mu
    var = jnp.mean(cen * cen, axis=-1, keepdims=True)
    c = _silu(cen * lax.rsqrt(var + EPS) * lg_ref[...] + lb_ref[...])
    c = (c * _silu(proj(_O_CZ, D_MODEL))).astype(BF16)
    xq = proj(_O_XQ, MIX_W)
    xqg = xqg_ref[...] * (HEAD_W ** -0.5)
    xq = jnp.concatenate([_rms_rows(xq[:, h * HEAD_W:(h + 1) * HEAD_W], xqg) for h in range(N_HEADS)], axis=-1)
    ox = _cross_attn(xq.astype(BF16), _silu(proj(_O_XZ, MIX_W)), mk_ref[0], mv_ref[0])
    y_ref[0] = x + _dot(jnp.concatenate([c, ox], axis=-1), w_out_ref[...])
    tail = u_sc[tm:tm + CONV_HIST, :]
    u_sc[0:CONV_HIST, :] = tail
    tail_ref[0] = tail


def _odd_layer(x, hist, norm_g, w_in, conv_w, conv_b, ln_g, ln_b, xqg, mem_k, mem_v, w_out, tm):
    bsz, seq, _ = x.shape
    tm = min(tm, seq)
    tok = pl.BlockSpec((1, tm, D_MODEL), lambda b, i: (b, i, 0))
    hist_spec = pl.BlockSpec((1, CONV_HIST, D_MODEL), lambda b, i: (b, 0, 0))
    mem_spec = pl.BlockSpec((1, N_MEM, MIX_W), lambda b, i: (b, 0, 0))
    vec = _const_spec((1, D_MODEL))
    return pl.pallas_call(
        functools.partial(_odd_kernel, tm=tm),
        grid=(bsz, seq // tm),
        in_specs=[tok, hist_spec, vec, _const_spec(w_in.shape), _const_spec(conv_w.shape), vec, vec, vec,
                  _const_spec((1, HEAD_W)), mem_spec, mem_spec, _const_spec(w_out.shape)],
        out_specs=[tok, hist_spec],
        out_shape=[jax.ShapeDtypeStruct(x.shape, F32), jax.ShapeDtypeStruct((bsz, CONV_HIST, D_MODEL), F32)],
        scratch_shapes=[pltpu.VMEM((CONV_HIST + tm, D_MODEL), F32)],
        compiler_params=_params(("arbitrary", "arbitrary")),
        name="odd_layer",
    )(x, hist, norm_g, w_in, conv_w, conv_b, ln_g, ln_b, xqg, mem_k, mem_v, w_out)


def _even_layer(x, mem_k, mem_v, hist, wts, lam_init, tm_in, tm_out, chunk, tq):
    (norm_g, w_main, w_gate, gate_bias, grp, qn, kn, xqg, mlstm_g, slopes, lam_vecs, subln_g, w_out) = wts
    bsz, seq, _ = x.shape
    (qa, ka, va, ga, gates, qb, kb, kbh, vb, vbh, gb, xq, gx) = _in_even(
        x.reshape(bsz * seq, D_MODEL), norm_g, w_main, w_gate, gate_bias, grp, qn, kn, xqg, tm_in)
    tok = lambda a: a.reshape(bsz, seq, a.shape[-1])
    if hist is None:
        state0 = jnp.zeros((bsz, N_HEADS, HEAD_W, 2 * HEAD_W), F32)
        m0 = jnp.zeros((bsz, N_HEADS, 1, HEAD_W), F32)
    else:
        k_past, v_past, c0, n0, m0 = hist
        state0 = jnp.concatenate([c0, n0[..., None], jnp.zeros(c0.shape[:-1] + (HEAD_W - 1,), F32)], axis=-1)
        m0 = jnp.broadcast_to(m0[..., None, None], (bsz, N_HEADS, 1, HEAD_W))
    oa, state1, m1 = _mlstm(tok(qa), tok(ka), tok(va), tok(gates), tok(ga), mlstm_g, state0, m0, chunk)
    if hist is None:
        ob = _diff_prompt(tok(qb), tok(kbh), tok(vbh), tok(gb), slopes, lam_vecs, subln_g, lam_init, tq)
    else:
        past = k_past.shape[1]
        k_all = jnp.concatenate([k_past.reshape(bsz, past, MIX_W).astype(BF16), tok(kbh)], axis=1)
        v_all = jnp.concatenate([v_past.reshape(bsz, past, MIX_W).astype(BF16), tok(vbh)], axis=1)
        ob = _diff_sample(tok(qb), k_all, v_all, tok(gb), slopes, lam_vecs, subln_g, lam_init)
    y = _out_even(x, oa, ob, tok(xq), tok(gx), mem_k, mem_v, w_out, tm_out)
    return (y, kb.reshape(bsz, seq, N_HEADS, HEAD_W), vb.reshape(bsz, seq, N_HEADS, HEAD_W),
            state1[..., :HEAD_W], state1[..., HEAD_W], m1[:, :, 0, 0])


def kernel(x_prompt, x_sample, mem_prompt, cache_xk, cache_xv, cache_k, cache_v, state_C, state_n, state_m,
           state_conv, norm_g, w_in_a, b_ig, b_fg, mlstm_norm_g, qn_g, kn_g, lam_q1, lam_k1, lam_q2, lam_k2,
           subln_g, w_out_a, w_in_c, conv_w, conv_b, conv_ln_g, conv_ln_b, w_out_c, mem_norm_g, w_mem_kv,
           xq_norm_g, xk_norm_g):
    depth = norm_g.shape[0]
    bsz = x_prompt.shape[0]
    dec = x_sample.shape[0]
    p_xk, p_xv = _mem_kv(mem_prompt, mem_norm_g, w_mem_kv, xk_norm_g)
    mem_k_p, mem_v_p = p_xk.astype(BF16), p_xv.astype(BF16)
    mem_k_s = cache_xk.reshape(depth, dec, N_MEM, MIX_W).astype(BF16)
    mem_v_s = cache_xv.reshape(depth, dec, N_MEM, MIX_W).astype(BF16)

    lane_grp = jnp.arange(MIX_W) // DK_B
    grp = (lane_grp[:, None] == lane_grp[None, :]).astype(BF16)
    slopes = jnp.broadcast_to(jnp.array(ALIBI_SLOPES, F32)[:, None, None], (N_HEADS, 1, HEAD_W))
    n_gate = 2 * N_HEADS
    gate0 = 5 * MIX_W

    yp, ys = x_prompt, x_sample
    outs = {name: [] for name in ("p_k", "p_v", "p_C", "p_n", "p_m", "p_conv",
                                  "s_k", "s_v", "s_C", "s_n", "s_m", "s_conv")}
    for layer in range(depth):
        if layer % 2 == 0:
            e = layer // 2
            w = w_in_a[e]
            w_main = jnp.concatenate([w[:, :gate0], w[:, gate0 + n_gate:]], axis=1).astype(BF16)
            w_gate = jnp.pad(w[:, gate0:gate0 + n_gate], ((0, 0), (0, HEAD_W - n_gate))).astype(BF16)
            gate_bias = jnp.pad(jnp.concatenate([b_ig[e], b_fg[e]]), (0, HEAD_W - n_gate)).reshape(1, HEAD_W)
            wts = (norm_g[layer].reshape(1, D_MODEL), w_main, w_gate, gate_bias, grp,
                   jnp.tile(qn_g[e], MIX_W // DK_B).reshape(1, MIX_W),
                   jnp.tile(kn_g[e], MIX_W // DK_B).reshape(1, MIX_W),
                   xq_norm_g[layer].reshape(1, HEAD_W), mlstm_norm_g[e].reshape(1, MIX_W), slopes,
                   tuple(v[e].reshape(1, DK_B) for v in (lam_q1, lam_k1, lam_q2, lam_k2)),
                   subln_g[e].reshape(1, HEAD_W), w_out_a[e].astype(BF16))
            lam_init = _lambda_init(layer)
            yp, k_new, v_new, c1, n1, m1 = _even_layer(yp, mem_k_p[layer], mem_v_p[layer], None, wts, lam_init,
                                                       tm_in=256, tm_out=512, chunk=128, tq=256)
            for name, val in zip(("p_k", "p_v", "p_C", "p_n", "p_m"), (k_new, v_new, c1, n1, m1)):
                outs[name].append(val)
            hist = (cache_k[e], cache_v[e], state_C[e], state_n[e], state_m[e])
            ys, k_new, v_new, c1, n1, m1 = _even_layer(ys, mem_k_s[layer], mem_v_s[layer], hist, wts, lam_init,
                                                       tm_in=256, tm_out=512, chunk=128, tq=256)
            for name, val in zip(("s_k", "s_v", "s_C", "s_n", "s_m"), (k_new, v_new, c1, n1, m1)):
                outs[name].append(val)
        else:
            o = layer // 2
            vec = lambda a: a.reshape(1, -1)
            wts = (vec(norm_g[layer]), w_in_c[o].astype(BF16), jnp.pad(conv_w[o], ((0, 1), (0, 0))),
                   vec(conv_b[o]), vec(conv_ln_g[o]), vec(conv_ln_b[o]), vec(xq_norm_g[layer]))
            w_out = w_out_c[o].astype(BF16)
            pad = CONV_HIST - (CONV_W - 1)
            zero_hist = jnp.zeros((bsz, CONV_HIST, D_MODEL), F32)
            yp, tail = _odd_layer(yp, zero_hist, *wts, mem_k_p[layer], mem_v_p[layer], w_out, tm=256)
            outs["p_conv"].append(tail[:, pad:])
            hist = jnp.pad(state_conv[o], ((0, 0), (pad, 0), (0, 0)))
            ys, tail = _odd_layer(ys, hist, *wts, mem_k_s[layer], mem_v_s[layer], w_out, tm=256)
            outs["s_conv"].append(tail[:, pad:])

    head5 = lambda a: a.reshape(a.shape[:-1] + (N_HEADS, HEAD_W))
    st = {name: jnp.stack(vals) for name, vals in outs.items()}
    return (yp, ys, head5(p_xk), head5(p_xv), st["p_k"], st["p_v"], st["p_C"], st["p_n"], st["p_m"], st["p_conv"],
            st["s_k"], st["s_v"], st["s_C"], st["s_n"], st["s_m"], st["s_conv"])
```

```python
import functools
import math

import jax
import jax.numpy as jnp
from jax import lax
from jax.experimental import pallas as pl
from jax.experimental.pallas import tpu as pltpu

F32 = jnp.float32
BF16 = jnp.bfloat16

D_MODEL = 1024
CHUNK = 64
EPS = 1e-6
N_HEADS = 4
HEAD_W = 128
MIX_W = N_HEADS * HEAD_W
DK_B = 64
CONV_W = 31
CONV_HIST = 32
CONV_ROWS = 64
SUBLANES = 8
N_MEM = 256
ALIBI_SLOPES = tuple(2.0 ** (-8.0 * (h + 1) / N_HEADS) for h in range(N_HEADS))
VMEM_LIMIT = 56 * 1024 * 1024

NEG_INF = float("-inf")
LOG2E = math.log2(math.e)
ACC_ROWS = HEAD_W + 16


def _lambda_init(layer):
    return 0.8 - 0.6 * math.exp(-0.3 * layer)


def _params(sem):
    return pltpu.CompilerParams(dimension_semantics=sem, vmem_limit_bytes=VMEM_LIMIT)


def _const_spec(shape):
    zeros = (0,) * len(shape)
    return pl.BlockSpec(shape, lambda *_: zeros)


def _rms_rows(x, g):
    return x * lax.rsqrt(jnp.mean(x * x, axis=-1, keepdims=True) + EPS) * g


def _sigmoid(x):
    return 1.0 / (1.0 + jnp.exp2(x * (-LOG2E)))


def _silu(x):
    return x * _sigmoid(x)


def _dot(a, b):
    return jnp.dot(a, b, preferred_element_type=F32)


def _dot_nt(a, b):
    return lax.dot_general(a, b, (((1,), (1,)), ((), ())), preferred_element_type=F32)


def _dot_tn(a, b):
    return lax.dot_general(a, b, (((0,), (0,)), ((), ())), preferred_element_type=F32)


def _ones_col(rows):
    lane = lax.broadcasted_iota(jnp.int32, (rows, HEAD_W), 1)
    return jnp.where(lane == 0, 1.0, 0.0).astype(BF16)


def _mem_kv_kernel(mem_ref, g_ref, w_ref, kg_ref, k_ref, v_ref):
    xn = _rms_rows(mem_ref[0], g_ref[0]).astype(BF16)
    kv = _dot(xn, w_ref[0])
    kg = kg_ref[0]
    for h in range(N_HEADS):
        sl = slice(h * HEAD_W, (h + 1) * HEAD_W)
        k_ref[0, 0, :, sl] = _rms_rows(kv[:, sl], kg)
    v_ref[0, 0] = kv[:, MIX_W:]


def _mem_kv(mem, g, w_kv, kg):
    depth = w_kv.shape[0]
    bsz, n_mem, _ = mem.shape
    out = jax.ShapeDtypeStruct((depth, bsz, n_mem, MIX_W), F32)
    return pl.pallas_call(
        _mem_kv_kernel,
        grid=(depth, bsz),
        in_specs=[
            pl.BlockSpec((1, n_mem, D_MODEL), lambda l, b: (b, 0, 0)),
            pl.BlockSpec((1, 1, D_MODEL), lambda l, b: (l, 0, 0)),
            pl.BlockSpec((1, D_MODEL, 2 * MIX_W), lambda l, b: (l, 0, 0)),
            pl.BlockSpec((1, 1, HEAD_W), lambda l, b: (l, 0, 0)),
        ],
        out_specs=[pl.BlockSpec((1, 1, n_mem, MIX_W), lambda l, b: (l, b, 0, 0))] * 2,
        out_shape=[out, out],
        compiler_params=_params(("arbitrary", "arbitrary")),
        name="mem_kv",
    )(mem, g.reshape(depth, 1, D_MODEL), w_kv.astype(BF16), kg.reshape(depth, 1, HEAD_W))


_E_AQ, _E_AK, _E_AV, _E_AO, _E_AZ, _E_BQ, _E_BK, _E_BV, _E_BZ, _E_XQ, _E_XZ = (i * MIX_W for i in range(11))
_E_MAIN = 11 * MIX_W


def _group_mean_sq(y, grp):
    sq = y * y
    hi = sq.astype(BF16)
    lo = (sq - hi.astype(F32)).astype(BF16)
    return (_dot(hi, grp) + _dot(lo, grp)) * (1.0 / DK_B)


def _in_even_kernel(x_ref, g_ref, w_ref, wg_ref, gbias_ref, grp_ref, qn_ref, kn_ref, xqg_ref,
                    qa_ref, ka_ref, va_ref, ga_ref, gt_ref, qb_ref, kb_ref, kbh_ref, vb_ref, vbh_ref,
                    gb_ref, xq_ref, gx_ref):
    xn = _rms_rows(x_ref[...], g_ref[...]).astype(BF16)

    def proj(c0):
        return _dot(xn, w_ref[:, c0:c0 + MIX_W])

    qa_ref[...] = proj(_E_AQ).astype(BF16)
    ka_ref[...] = (proj(_E_AK) * (HEAD_W ** -0.5)).astype(BF16)
    va_ref[...] = proj(_E_AV).astype(BF16)
    ga_ref[...] = (_sigmoid(proj(_E_AO)) * _silu(proj(_E_AZ))).astype(BF16)
    gt = _dot(xn, wg_ref[...]) + gbias_ref[...]
    lane = lax.broadcasted_iota(jnp.int32, gt.shape, 1)
    log_sig = jnp.minimum(gt, 0.0) - jnp.log1p(jnp.exp(-jnp.abs(gt)))
    gt_ref[...] = jnp.where((lane >= N_HEADS) & (lane < 2 * N_HEADS), log_sig, gt)
    grp = grp_ref[...]
    bq = proj(_E_BQ)
    q_scale = (DK_B ** -0.5) * LOG2E
    qb_ref[...] = (bq * lax.rsqrt(_group_mean_sq(bq, grp) + EPS) * qn_ref[...] * q_scale).astype(BF16)
    bk = proj(_E_BK)
    kb = bk * lax.rsqrt(_group_mean_sq(bk, grp) + EPS) * kn_ref[...]
    kb_ref[...] = kb
    kbh_ref[...] = kb.astype(BF16)
    bv = proj(_E_BV)
    vb_ref[...] = bv
    vbh_ref[...] = bv.astype(BF16)
    gb_ref[...] = _silu(proj(_E_BZ)).astype(BF16)
    xq = proj(_E_XQ)
    xqg = xqg_ref[...] * ((HEAD_W ** -0.5) * LOG2E)
    for h in range(N_HEADS):
        sl = slice(h * HEAD_W, (h + 1) * HEAD_W)
        xq_ref[:, sl] = _rms_rows(xq[:, sl], xqg).astype(BF16)
    gx_ref[...] = _silu(proj(_E_XZ)).astype(BF16)


def _in_even(x2d, norm_g, w_main, w_gate, gate_bias, grp, qn, kn, xqg, tm):
    m = x2d.shape[0]
    tm = min(tm, m)
    row = lambda width: pl.BlockSpec((tm, width), lambda i: (i, 0))
    half = jax.ShapeDtypeStruct((m, MIX_W), BF16)
    full = jax.ShapeDtypeStruct((m, MIX_W), F32)
    out_shape = [half, half, half, half, jax.ShapeDtypeStruct((m, HEAD_W), F32),
                 half, full, half, full, half, half, half, half]
    out_specs = [row(MIX_W)] * 4 + [row(HEAD_W)] + [row(MIX_W)] * 8
    return pl.pallas_call(
        _in_even_kernel,
        grid=(m // tm,),
        in_specs=[
            row(D_MODEL),
            _const_spec((1, D_MODEL)),
            _const_spec((D_MODEL, _E_MAIN)),
            _const_spec((D_MODEL, HEAD_W)),
            _const_spec((1, HEAD_W)),
            _const_spec((MIX_W, MIX_W)),
            _const_spec((1, MIX_W)),
            _const_spec((1, MIX_W)),
            _const_spec((1, HEAD_W)),
        ],
        out_specs=out_specs,
        out_shape=out_shape,
        compiler_params=_params(("arbitrary",)),
        name="in_even",
    )(x2d, norm_g, w_main, w_gate, gate_bias, grp, qn, kn, xqg)


def _mlstm_kernel(q_ref, k_ref, v_ref, gt_ref, ga_ref, ng_ref, s0_ref, m0_ref, o_ref, s_ref, m_ref, *, chunk):
    c = pl.program_id(1)

    @pl.when(c == 0)
    def _():
        s_ref[...] = s0_ref[...]
        m_ref[...] = m0_ref[...]

    gt = gt_ref[0]
    row = lax.broadcasted_iota(jnp.int32, (chunk, chunk), 0)
    col = lax.broadcasted_iota(jnp.int32, (chunk, chunk), 1)
    causal = col <= row
    cum = jnp.dot(causal.astype(F32), gt, preferred_element_type=F32,
                  precision=lax.Precision.HIGHEST)
    gt_t = gt.T
    cum_t = cum.T
    ones = _ones_col(chunk)
    for h in range(N_HEADS):
        sl = slice(h * HEAD_W, (h + 1) * HEAD_W)
        b_col = cum[:, N_HEADS + h:N_HEADS + h + 1]
        i_col = gt[:, h:h + 1]
        r_row = gt_t[h:h + 1, :] - cum_t[N_HEADS + h:N_HEADS + h + 1, :]
        m0 = m_ref[0, h][:, 0:1]
        g_col = b_col + m0
        logw = jnp.where(causal, b_col + r_row, NEG_INF)
        m_col = jnp.maximum(g_col, jnp.max(logw, axis=-1, keepdims=True))
        w_intra = jnp.exp(logw - m_col)
        w_inter = jnp.exp(g_col - m_col)
        q = q_ref[0, :, sl]
        k = k_ref[0, :, sl]
        v_aug = jnp.concatenate([v_ref[0, :, sl], ones], axis=-1)
        state = s_ref[0, h]
        sc = (_dot_nt(q, k) * w_intra).astype(BF16)
        inter = _dot(q, state.astype(BF16))
        intra = _dot(sc, v_aug)
        num = w_inter * inter[:, :HEAD_W] + intra[:, :HEAD_W]
        den = w_inter * inter[:, HEAD_W:HEAD_W + 1] + intra[:, HEAD_W:HEAD_W + 1]
        hid = num / jnp.maximum(jnp.abs(den), jnp.exp(-m_col))
        out = _rms_rows(hid, ng_ref[:, sl]) * ga_ref[0, :, sl].astype(F32)
        o_ref[0, :, sl] = out.astype(BF16)
        m_last = m_col[chunk - 1:chunk, :]
        decay = jnp.exp(g_col[chunk - 1:chunk, :] - m_last)
        w_end = jnp.exp(b_col[chunk - 1:chunk, :] - b_col + i_col - m_last)
        kv = _dot_tn(k, (v_aug.astype(F32) * w_end).astype(BF16))
        s_ref[0, h] = decay * state + kv
        m_ref[0, h] = jnp.broadcast_to(m_last, (1, HEAD_W))


def _mlstm(q, k, v, gates, gate_a, norm_g, state0, m0, chunk):
    bsz, seq, _ = q.shape
    chunk = min(chunk, seq)
    tok = lambda width: pl.BlockSpec((1, chunk, width), lambda b, c: (b, c, 0))
    st_spec = pl.BlockSpec((1, N_HEADS, HEAD_W, 2 * HEAD_W), lambda b, c: (b, 0, 0, 0))
    m_spec = pl.BlockSpec((1, N_HEADS, 1, HEAD_W), lambda b, c: (b, 0, 0, 0))
    return pl.pallas_call(
        functools.partial(_mlstm_kernel, chunk=chunk),
        grid=(bsz, seq // chunk),
        in_specs=[tok(MIX_W), tok(MIX_W), tok(MIX_W), tok(HEAD_W), tok(MIX_W),
                  _const_spec((1, MIX_W)), st_spec, m_spec],
        out_specs=[tok(MIX_W), st_spec, m_spec],
        out_shape=[jax.ShapeDtypeStruct((bsz, seq, MIX_W), BF16),
                   jax.ShapeDtypeStruct(state0.shape, F32),
                   jax.ShapeDtypeStruct(m0.shape, F32)],
        compiler_params=_params(("arbitrary", "arbitrary")),
        name="mlstm",
    )(q, k, v, gates, gate_a, norm_g, state0, m0)


def _split_maps(q):
    lane = lax.broadcasted_iota(jnp.int32, q.shape, 1)
    zero = jnp.zeros_like(q)
    return jnp.concatenate([jnp.where(lane < DK_B, q, zero), jnp.where(lane >= DK_B, q, zero)], axis=0)


def _diff_lambda(lam_refs, lam_init):
    lq1, lk1, lq2, lk2 = (r[...] for r in lam_refs)
    return (jnp.exp(jnp.sum(lq1 * lk1, axis=-1, keepdims=True))
            - jnp.exp(jnp.sum(lq2 * lk2, axis=-1, keepdims=True)) + lam_init)


def _diff_gate(o, lam_init, sg, gate):
    return (_rms_rows(o, sg) * (1.0 - lam_init) * gate.astype(F32)).astype(BF16)


def _diff_prompt_kernel(q_ref, k_ref, vt_ref, gate_ref, base_ref, dbias_ref, lq1_ref, lk1_ref, lq2_ref, lk2_ref,
                        sg_ref, o_ref, m_sc, acc_sc, p_sc, *, tq, lam_init):
    i = pl.program_id(2)
    n_tiles = 2 * tq // HEAD_W
    qq = _split_maps(q_ref[0])
    m_sc[...] = jnp.full(m_sc.shape, NEG_INF, F32)
    acc_sc[...] = jnp.zeros(acc_sc.shape, F32)
    ones_rows = jnp.ones((ACC_ROWS - HEAD_W, tq), BF16)

    def block(j, bias_tile, offset):
        start = pl.multiple_of(j * tq, tq)
        s_t = _dot_nt(k_ref[0, pl.ds(start, tq), :], qq)
        alphas = []
        for t in range(n_tiles):
            cs = slice(t * HEAD_W, (t + 1) * HEAD_W)
            s = s_t[:, cs] + bias_tile(t)
            m_old = m_sc[:, cs]
            m_new = jnp.maximum(m_old, jnp.max(s, axis=0, keepdims=True) + offset)
            p_sc[:, cs] = jnp.exp2(s - (m_new - offset)).astype(BF16)
            alphas.append(jnp.exp2(m_old - m_new))
            m_sc[:, cs] = m_new
        v_aug = jnp.concatenate([vt_ref[0, :, pl.ds(start, tq)], ones_rows], axis=0)
        acc_sc[...] = jnp.concatenate(alphas, axis=-1) * acc_sc[...] + _dot(v_aug, p_sc[...])

    def earlier(j, carry):
        offset = base_ref[0, 1:2, 0:1] * ((j - i) * tq).astype(F32)
        block(j, lambda t: base_ref[0], offset)
        return carry

    lax.fori_loop(0, i, earlier, 0)
    tiles_per_map = tq // HEAD_W
    block(i, lambda t: dbias_ref[0, :, (t % tiles_per_map) * HEAD_W:(t % tiles_per_map + 1) * HEAD_W], 0.0)

    acc = acc_sc[...]
    lam = _diff_lambda((lq1_ref, lk1_ref, lq2_ref, lk2_ref), lam_init)
    o_t = (acc[:HEAD_W, :tq] / acc[HEAD_W:HEAD_W + 1, :tq]
           - lam * (acc[:HEAD_W, tq:] / acc[HEAD_W:HEAD_W + 1, tq:]))
    o_ref[0] = _diff_gate(o_t.T, lam_init, sg_ref[...], gate_ref[0])


def _alibi_tables(tq):
    slopes = jnp.array(ALIBI_SLOPES, F32) * LOG2E
    kk = jnp.arange(tq, dtype=jnp.int32)[:, None]
    qi = jnp.arange(tq, dtype=jnp.int32)[None, :]
    base = slopes[:, None, None] * jnp.broadcast_to(kk.astype(F32), (tq, HEAD_W))[None]
    diag = slopes[:, None, None] * (qi - jnp.abs(qi - kk)).astype(F32)[None]
    diag = jnp.where((kk // CHUNK <= qi // CHUNK)[None], diag, NEG_INF)
    return base, diag


def _diff_prompt(q, k, vt, gate, lam_vecs, subln_g, lam_init, tq):
    bsz, seq, _ = q.shape
    tq = min(tq, seq)
    base, diag = _alibi_tables(tq)
    q_spec = pl.BlockSpec((1, tq, HEAD_W), lambda b, h, i: (b, i, h))
    return pl.pallas_call(
        functools.partial(_diff_prompt_kernel, tq=tq, lam_init=lam_init),
        grid=(bsz, N_HEADS, seq // tq),
        in_specs=[q_spec,
                  pl.BlockSpec((1, seq, HEAD_W), lambda b, h, i: (b, 0, h)),
                  pl.BlockSpec((1, HEAD_W, seq), lambda b, h, i: (b, h, 0)),
                  q_spec,
                  pl.BlockSpec((1, tq, HEAD_W), lambda b, h, i: (h, 0, 0)),
                  pl.BlockSpec((1, tq, tq), lambda b, h, i: (h, 0, 0))]
                 + [_const_spec((1, DK_B))] * 4 + [_const_spec((1, HEAD_W))],
        out_specs=q_spec,
        out_shape=jax.ShapeDtypeStruct((bsz, seq, MIX_W), BF16),
        scratch_shapes=[pltpu.VMEM((1, 2 * tq), F32), pltpu.VMEM((ACC_ROWS, 2 * tq), F32),
                        pltpu.VMEM((tq, 2 * tq), BF16)],
        compiler_params=_params(("arbitrary", "arbitrary", "arbitrary")),
        name="diff_prompt",
    )(q, k, vt, gate, base, diag, *lam_vecs, subln_g)


def _diff_sample_kernel(q_ref, k_ref, v_ref, gate_ref, slope_ref, lq1_ref, lk1_ref, lq2_ref, lk2_ref, sg_ref,
                        o_ref, *, past, lam_init):
    rows = q_ref.shape[1]
    keys = k_ref.shape[1]
    slope = slope_ref[0][:, 0:1]
    qq = _split_maps(q_ref[0])
    r = lax.broadcasted_iota(jnp.int32, (2 * rows, keys), 0)
    r = jnp.where(r >= rows, r - rows, r) + past
    c = lax.broadcasted_iota(jnp.int32, (2 * rows, keys), 1)
    s = _dot_nt(qq, k_ref[0]) - slope * jnp.abs(r - c).astype(F32)
    p = jnp.exp2(s - jnp.max(s, axis=-1, keepdims=True))
    v_aug = jnp.concatenate([v_ref[0], _ones_col(keys)], axis=-1)
    acc = _dot(p.astype(BF16), v_aug)
    lam = _diff_lambda((lq1_ref, lk1_ref, lq2_ref, lk2_ref), lam_init)
    o = (acc[:rows, :HEAD_W] / acc[:rows, HEAD_W:HEAD_W + 1]
         - lam * (acc[rows:, :HEAD_W] / acc[rows:, HEAD_W:HEAD_W + 1]))
    o_ref[0] = _diff_gate(o, lam_init, sg_ref[...], gate_ref[0])


def _diff_sample(q, k_all, v_all, gate, slopes, lam_vecs, subln_g, lam_init):
    bsz, rows, _ = q.shape
    keys = k_all.shape[1]
    q_spec = pl.BlockSpec((1, rows, HEAD_W), lambda b, h: (b, 0, h))
    kv_spec = pl.BlockSpec((1, keys, HEAD_W), lambda b, h: (b, 0, h))
    return pl.pallas_call(
        functools.partial(_diff_sample_kernel, past=keys - rows, lam_init=lam_init),
        grid=(bsz, N_HEADS),
        in_specs=[q_spec, kv_spec, kv_spec, q_spec,
                  pl.BlockSpec((1, 1, HEAD_W), lambda b, h: (h, 0, 0))]
                 + [_const_spec((1, DK_B))] * 4 + [_const_spec((1, HEAD_W))],
        out_specs=q_spec,
        out_shape=jax.ShapeDtypeStruct((bsz, rows, MIX_W), BF16),
        compiler_params=_params(("arbitrary", "arbitrary")),
        name="diff_sample",
    )(q, k_all, v_all, gate, slopes, *lam_vecs, subln_g)


def _cross_attn(xq, gate, mem_k, mem_v):
    outs = []
    for h in range(N_HEADS):
        sl = slice(h * HEAD_W, (h + 1) * HEAD_W)
        s = _dot_nt(xq[:, sl], mem_k[:, sl])
        p = jnp.exp2(s - jnp.max(s, axis=-1, keepdims=True))
        outs.append(_dot(p.astype(BF16), mem_v[:, sl]) / jnp.sum(p, axis=-1, keepdims=True))
    return (jnp.concatenate(outs, axis=-1) * gate).astype(BF16)


def _out_even_kernel(x_ref, oa_ref, ob_ref, xq_ref, gx_ref, mk_ref, mv_ref, w_ref, y_ref):
    ox = _cross_attn(xq_ref[0], gx_ref[0].astype(F32), mk_ref[0], mv_ref[0])
    mixed = jnp.concatenate([oa_ref[0], ob_ref[0], ox], axis=-1)
    y_ref[0] = x_ref[0] + _dot(mixed, w_ref[...])


def _out_even(x, oa, ob, xq, gx, mem_k, mem_v, w_out, tm):
    bsz, seq, _ = x.shape
    tm = min(tm, seq)
    tok = lambda width: pl.BlockSpec((1, tm, width), lambda b, i: (b, i, 0))
    mem_spec = pl.BlockSpec((1, N_MEM, MIX_W), lambda b, i: (b, 0, 0))
    return pl.pallas_call(
        _out_even_kernel,
        grid=(bsz, seq // tm),
        in_specs=[tok(D_MODEL), tok(MIX_W), tok(MIX_W), tok(MIX_W), tok(MIX_W), mem_spec, mem_spec,
                  _const_spec((3 * MIX_W, D_MODEL))],
        out_specs=tok(D_MODEL),
        out_shape=jax.ShapeDtypeStruct(x.shape, F32),
        compiler_params=_params(("arbitrary", "arbitrary")),
        name="out_even",
    )(x, oa, ob, xq, gx, mem_k, mem_v, w_out)


_O_CU, _O_CG, _O_CZ = 0, D_MODEL, 2 * D_MODEL
_O_XQ, _O_XZ = 3 * D_MODEL, 3 * D_MODEL + MIX_W


def _odd_kernel(x_ref, hist_ref, g_ref, w_in_ref, cw_ref, cb_ref, lg_ref, lb_ref, xqg_ref, mk_ref, mv_ref,
                w_out_ref, y_ref, tail_ref, u_sc, conv_sc, *, tm):
    i = pl.program_id(1)

    @pl.when(i == 0)
    def _():
        u_sc[0:CONV_HIST, :] = hist_ref[0]

    x = x_ref[0]
    xn = _rms_rows(x, g_ref[...]).astype(BF16)

    def proj(c0, width):
        return _dot(xn, w_in_ref[:, c0:c0 + width])

    u = proj(_O_CU, D_MODEL) * _sigmoid(proj(_O_CG, D_MODEL))
    u_sc[CONV_HIST:CONV_HIST + tm, :] = u
    rows = min(CONV_ROWS, tm)
    for c0 in range(0, D_MODEL, HEAD_W):
        cs = slice(c0, c0 + HEAD_W)
        for r0 in range(0, tm, rows):
            acc = jnp.broadcast_to(cb_ref[:, cs], (rows, HEAD_W))
            span = rows + CONV_HIST
            aligned = u_sc[r0:r0 + span, cs]
            for phase in range(SUBLANES):
                offs = [o for o in range(CONV_HIST - (CONV_W - 1), CONV_HIST + 1) if o % SUBLANES == phase]
                win = pltpu.roll(aligned, span - phase, axis=0) if phase else aligned
                for off in offs:
                    j = off - (CONV_HIST - (CONV_W - 1))
                    acc = acc + cw_ref[j:j + 1, cs] * win[off - phase:off - phase + rows]
            conv_sc[r0:r0 + rows, cs] = acc
    acc = conv_sc[...]
    mu = jnp.mean(acc, axis=-1, keepdims=True)
    cen = acc - mu
    var = jnp.mean(cen * cen, axis=-1, keepdims=True)
    c = _silu(cen * lax.rsqrt(var + EPS) * lg_ref[...] + lb_ref[...])
    c = (c * _silu(proj(_O_CZ, D_MODEL))).astype(BF16)
    xq = proj(_O_XQ, MIX_W)
    xqg = xqg_ref[...] * ((HEAD_W ** -0.5) * LOG2E)
    xq = jnp.concatenate([_rms_rows(xq[:, h * HEAD_W:(h + 1) * HEAD_W], xqg) for h in range(N_HEADS)], axis=-1)
    ox = _cross_attn(xq.astype(BF16), _silu(proj(_O_XZ, MIX_W)), mk_ref[0], mv_ref[0])
    y_ref[0] = x + _dot(jnp.concatenate([c, ox], axis=-1), w_out_ref[...])
    tail = u_sc[tm:tm + CONV_HIST, :]
    u_sc[0:CONV_HIST, :] = tail
    tail_ref[0] = tail


def _odd_layer(x, hist, norm_g, w_in, conv_w, conv_b, ln_g, ln_b, xqg, mem_k, mem_v, w_out, tm):
    bsz, seq, _ = x.shape
    tm = min(tm, seq)
    tok = pl.BlockSpec((1, tm, D_MODEL), lambda b, i: (b, i, 0))
    hist_spec = pl.BlockSpec((1, CONV_HIST, D_MODEL), lambda b, i: (b, 0, 0))
    mem_spec = pl.BlockSpec((1, N_MEM, MIX_W), lambda b, i: (b, 0, 0))
    vec = _const_spec((1, D_MODEL))
    return pl.pallas_call(
        functools.partial(_odd_kernel, tm=tm),
        grid=(bsz, seq // tm),
        in_specs=[tok, hist_spec, vec, _const_spec(w_in.shape), _const_spec(conv_w.shape), vec, vec, vec,
                  _const_spec((1, HEAD_W)), mem_spec, mem_spec, _const_spec(w_out.shape)],
        out_specs=[tok, hist_spec],
        out_shape=[jax.ShapeDtypeStruct(x.shape, F32), jax.ShapeDtypeStruct((bsz, CONV_HIST, D_MODEL), F32)],
        scratch_shapes=[pltpu.VMEM((CONV_HIST + tm, D_MODEL), F32), pltpu.VMEM((tm, D_MODEL), F32)],
        compiler_params=_params(("arbitrary", "arbitrary")),
        name="odd_layer",
    )(x, hist, norm_g, w_in, conv_w, conv_b, ln_g, ln_b, xqg, mem_k, mem_v, w_out)


def _even_layer(x, mem_k, mem_v, hist, wts, lam_init, tm_in, tm_out, chunk, tq):
    (norm_g, w_main, w_gate, gate_bias, grp, qn, kn, xqg, mlstm_g, slopes, lam_vecs, subln_g, w_out) = wts
    bsz, seq, _ = x.shape
    (qa, ka, va, ga, gates, qb, kb, kbh, vb, vbh, gb, xq, gx) = _in_even(
        x.reshape(bsz * seq, D_MODEL), norm_g, w_main, w_gate, gate_bias, grp, qn, kn, xqg, tm_in)
    tok = lambda a: a.reshape(bsz, seq, a.shape[-1])
    if hist is None:
        state0 = jnp.zeros((bsz, N_HEADS, HEAD_W, 2 * HEAD_W), F32)
        m0 = jnp.zeros((bsz, N_HEADS, 1, HEAD_W), F32)
    else:
        k_past, v_past, c0, n0, m0 = hist
        state0 = jnp.concatenate([c0, n0[..., None], jnp.zeros(c0.shape[:-1] + (HEAD_W - 1,), F32)], axis=-1)
        m0 = jnp.broadcast_to(m0[..., None, None], (bsz, N_HEADS, 1, HEAD_W))
    oa, state1, m1 = _mlstm(tok(qa), tok(ka), tok(va), tok(gates), tok(ga), mlstm_g, state0, m0, chunk)
    if hist is None:
        ob = _diff_prompt(tok(qb), tok(kbh), jnp.swapaxes(tok(vbh), 1, 2), tok(gb), lam_vecs, subln_g, lam_init, tq)
    else:
        past = k_past.shape[1]
        k_all = jnp.concatenate([k_past.reshape(bsz, past, MIX_W).astype(BF16), tok(kbh)], axis=1)
        v_all = jnp.concatenate([v_past.reshape(bsz, past, MIX_W).astype(BF16), tok(vbh)], axis=1)
        ob = _diff_sample(tok(qb), k_all, v_all, tok(gb), slopes, lam_vecs, subln_g, lam_init)
    y = _out_even(x, oa, ob, tok(xq), tok(gx), mem_k, mem_v, w_out, tm_out)
    return (y, kb.reshape(bsz, seq, N_HEADS, HEAD_W), vb.reshape(bsz, seq, N_HEADS, HEAD_W),
            state1[..., :HEAD_W], state1[..., HEAD_W], m1[:, :, 0, 0])


def kernel(x_prompt, x_sample, mem_prompt, cache_xk, cache_xv, cache_k, cache_v, state_C, state_n, state_m,
           state_conv, norm_g, w_in_a, b_ig, b_fg, mlstm_norm_g, qn_g, kn_g, lam_q1, lam_k1, lam_q2, lam_k2,
           subln_g, w_out_a, w_in_c, conv_w, conv_b, conv_ln_g, conv_ln_b, w_out_c, mem_norm_g, w_mem_kv,
           xq_norm_g, xk_norm_g):
    depth = norm_g.shape[0]
    bsz = x_prompt.shape[0]
    dec = x_sample.shape[0]
    p_xk, p_xv = _mem_kv(mem_prompt, mem_norm_g, w_mem_kv, xk_norm_g)
    mem_k_p, mem_v_p = p_xk.astype(BF16), p_xv.astype(BF16)
    mem_k_s = cache_xk.reshape(depth, dec, N_MEM, MIX_W).astype(BF16)
    mem_v_s = cache_xv.reshape(depth, dec, N_MEM, MIX_W).astype(BF16)

    lane_grp = jnp.arange(MIX_W) // DK_B
    grp = (lane_grp[:, None] == lane_grp[None, :]).astype(BF16)
    slopes = jnp.broadcast_to(jnp.array(ALIBI_SLOPES, F32)[:, None, None] * LOG2E, (N_HEADS, 1, HEAD_W))
    n_gate = 2 * N_HEADS
    gate0 = 5 * MIX_W

    yp, ys = x_prompt, x_sample
    outs = {name: [] for name in ("p_k", "p_v", "p_C", "p_n", "p_m", "p_conv",
                                  "s_k", "s_v", "s_C", "s_n", "s_m", "s_conv")}
    for layer in range(depth):
        if layer % 2 == 0:
            e = layer // 2
            w = w_in_a[e]
            w_main = jnp.concatenate([w[:, :gate0], w[:, gate0 + n_gate:]], axis=1).astype(BF16)
            w_gate = jnp.pad(w[:, gate0:gate0 + n_gate], ((0, 0), (0, HEAD_W - n_gate))).astype(BF16)
            gate_bias = jnp.pad(jnp.concatenate([b_ig[e], b_fg[e]]), (0, HEAD_W - n_gate)).reshape(1, HEAD_W)
            wts = (norm_g[layer].reshape(1, D_MODEL), w_main, w_gate, gate_bias, grp,
                   jnp.tile(qn_g[e], MIX_W // DK_B).reshape(1, MIX_W),
                   jnp.tile(kn_g[e], MIX_W // DK_B).reshape(1, MIX_W),
                   xq_norm_g[layer].reshape(1, HEAD_W), mlstm_norm_g[e].reshape(1, MIX_W), slopes,
                   tuple(v[e].reshape(1, DK_B) for v in (lam_q1, lam_k1, lam_q2, lam_k2)),
                   subln_g[e].reshape(1, HEAD_W), w_out_a[e].astype(BF16))
            lam_init = _lambda_init(layer)
            yp, k_new, v_new, c1, n1, m1 = _even_layer(yp, mem_k_p[layer], mem_v_p[layer], None, wts, lam_init,
                                                       tm_in=256, tm_out=512, chunk=128, tq=512)
            for name, val in zip(("p_k", "p_v", "p_C", "p_n", "p_m"), (k_new, v_new, c1, n1, m1)):
                outs[name].append(val)
            hist = (cache_k[e], cache_v[e], state_C[e], state_n[e], state_m[e])
            ys, k_new, v_new, c1, n1, m1 = _even_layer(ys, mem_k_s[layer], mem_v_s[layer], hist, wts, lam_init,
                                                       tm_in=256, tm_out=512, chunk=128, tq=512)
            for name, val in zip(("s_k", "s_v", "s_C", "s_n", "s_m"), (k_new, v_new, c1, n1, m1)):
                outs[name].append(val)
        else:
            o = layer // 2
            vec = lambda a: a.reshape(1, -1)
            wts = (vec(norm_g[layer]), w_in_c[o].astype(BF16), jnp.pad(conv_w[o], ((0, 1), (0, 0))),
                   vec(conv_b[o]), vec(conv_ln_g[o]), vec(conv_ln_b[o]), vec(xq_norm_g[layer]))
            w_out = w_out_c[o].astype(BF16)
            pad = CONV_HIST - (CONV_W - 1)
            zero_hist = jnp.zeros((bsz, CONV_HIST, D_MODEL), F32)
            yp, tail = _odd_layer(yp, zero_hist, *wts, mem_k_p[layer], mem_v_p[layer], w_out, tm=256)
            outs["p_conv"].append(tail[:, pad:])
            hist = jnp.pad(state_conv[o], ((0, 0), (pad, 0), (0, 0)))
            ys, tail = _odd_layer(ys, hist, *wts, mem_k_s[layer], mem_v_s[layer], w_out, tm=256)
            outs["s_conv"].append(tail[:, pad:])

    head5 = lambda a: a.reshape(a.shape[:-1] + (N_HEADS, HEAD_W))
    st = {name: jnp.stack(vals) for name, vals in outs.items()}
    return (yp, ys, head5(p_xk), head5(p_xv), st["p_k"], st["p_v"], st["p_C"], st["p_n"], st["p_m"], st["p_conv"],
            st["s_k"], st["s_v"], st["s_C"], st["s_n"], st["s_m"], st["s_conv"])
```

```python
import functools
import math

import jax
import jax.numpy as jnp
from jax import lax
from jax.experimental import pallas as pl
from jax.experimental.pallas import tpu as pltpu

F32 = jnp.float32
BF16 = jnp.bfloat16

D_MODEL = 1024
CHUNK = 64
EPS = 1e-6
N_HEADS = 4
HEAD_W = 128
MIX_W = N_HEADS * HEAD_W
DK_B = 64
CONV_W = 31
CONV_HIST = 32
CONV_ROWS = 64
SUBLANES = 8
MLSTM_ROWS = 1
N_MEM = 256
ALIBI_SLOPES = tuple(2.0 ** (-8.0 * (h + 1) / N_HEADS) for h in range(N_HEADS))
VMEM_LIMIT = 56 * 1024 * 1024

NEG_INF = float("-inf")
LOG2E = math.log2(math.e)
ACC_ROWS = HEAD_W + 16


def _lambda_init(layer):
    return 0.8 - 0.6 * math.exp(-0.3 * layer)


def _params(sem):
    return pltpu.CompilerParams(dimension_semantics=sem, vmem_limit_bytes=VMEM_LIMIT)


def _const_spec(shape):
    zeros = (0,) * len(shape)
    return pl.BlockSpec(shape, lambda *_: zeros)


def _rms_rows(x, g):
    return x * lax.rsqrt(jnp.mean(x * x, axis=-1, keepdims=True) + EPS) * g


def _sigmoid(x):
    return 1.0 / (1.0 + jnp.exp2(x * (-LOG2E)))


def _silu(x):
    return x * _sigmoid(x)


def _dot(a, b):
    return jnp.dot(a, b, preferred_element_type=F32)


def _dot_nt(a, b):
    return lax.dot_general(a, b, (((1,), (1,)), ((), ())), preferred_element_type=F32)


def _dot_tn(a, b):
    return lax.dot_general(a, b, (((0,), (0,)), ((), ())), preferred_element_type=F32)


def _ones_col(rows):
    lane = lax.broadcasted_iota(jnp.int32, (rows, HEAD_W), 1)
    return jnp.where(lane == 0, 1.0, 0.0).astype(BF16)


def _mem_kv_kernel(mem_ref, g_ref, w_ref, kg_ref, k_ref, v_ref):
    xn = _rms_rows(mem_ref[0], g_ref[0]).astype(BF16)
    kv = _dot(xn, w_ref[0])
    kg = kg_ref[0]
    for h in range(N_HEADS):
        sl = slice(h * HEAD_W, (h + 1) * HEAD_W)
        k_ref[0, 0, :, sl] = _rms_rows(kv[:, sl], kg)
    v_ref[0, 0] = kv[:, MIX_W:]


def _mem_kv(mem, g, w_kv, kg):
    depth = w_kv.shape[0]
    bsz, n_mem, _ = mem.shape
    out = jax.ShapeDtypeStruct((depth, bsz, n_mem, MIX_W), F32)
    return pl.pallas_call(
        _mem_kv_kernel,
        grid=(depth, bsz),
        in_specs=[
            pl.BlockSpec((1, n_mem, D_MODEL), lambda l, b: (b, 0, 0)),
            pl.BlockSpec((1, 1, D_MODEL), lambda l, b: (l, 0, 0)),
            pl.BlockSpec((1, D_MODEL, 2 * MIX_W), lambda l, b: (l, 0, 0)),
            pl.BlockSpec((1, 1, HEAD_W), lambda l, b: (l, 0, 0)),
        ],
        out_specs=[pl.BlockSpec((1, 1, n_mem, MIX_W), lambda l, b: (l, b, 0, 0))] * 2,
        out_shape=[out, out],
        compiler_params=_params(("arbitrary", "arbitrary")),
        name="mem_kv",
    )(mem, g.reshape(depth, 1, D_MODEL), w_kv.astype(BF16), kg.reshape(depth, 1, HEAD_W))


_E_AQ, _E_AK, _E_AV, _E_AO, _E_AZ, _E_BQ, _E_BK, _E_BV, _E_BZ, _E_XQ, _E_XZ = (i * MIX_W for i in range(11))
_E_MAIN = 11 * MIX_W


def _group_mean_sq(y, grp):
    sq = y * y
    hi = sq.astype(BF16)
    lo = (sq - hi.astype(F32)).astype(BF16)
    return (_dot(hi, grp) + _dot(lo, grp)) * (1.0 / DK_B)


def _in_even_kernel(x_ref, g_ref, w_ref, wg_ref, gbias_ref, grp_ref, qn_ref, kn_ref, xqg_ref,
                    qa_ref, ka_ref, va_ref, ga_ref, gt_ref, qb_ref, kb_ref, kbh_ref, vb_ref, vbh_ref,
                    gb_ref, xq_ref, gx_ref):
    xn = _rms_rows(x_ref[...], g_ref[...]).astype(BF16)

    def proj(c0):
        return _dot(xn, w_ref[:, c0:c0 + MIX_W])

    qa_ref[...] = proj(_E_AQ).astype(BF16)
    ka_ref[...] = (proj(_E_AK) * (HEAD_W ** -0.5)).astype(BF16)
    va_ref[...] = proj(_E_AV).astype(BF16)
    ga_ref[...] = (_sigmoid(proj(_E_AO)) * _silu(proj(_E_AZ))).astype(BF16)
    gt = _dot(xn, wg_ref[...]) + gbias_ref[...]
    lane = lax.broadcasted_iota(jnp.int32, gt.shape, 1)
    log_sig = jnp.minimum(gt, 0.0) - jnp.log1p(jnp.exp(-jnp.abs(gt)))
    gt_ref[...] = jnp.where((lane >= N_HEADS) & (lane < 2 * N_HEADS), log_sig, gt)
    grp = grp_ref[...]
    bq = proj(_E_BQ)
    q_scale = (DK_B ** -0.5) * LOG2E
    qb_ref[...] = (bq * lax.rsqrt(_group_mean_sq(bq, grp) + EPS) * qn_ref[...] * q_scale).astype(BF16)
    bk = proj(_E_BK)
    kb = bk * lax.rsqrt(_group_mean_sq(bk, grp) + EPS) * kn_ref[...]
    kb_ref[...] = kb
    kbh_ref[...] = kb.astype(BF16)
    bv = proj(_E_BV)
    vb_ref[...] = bv
    if len(vbh_ref.shape) == 3:
        vbh_ref[0] = bv.T.astype(BF16)
    else:
        vbh_ref[...] = bv.astype(BF16)
    gb_ref[...] = _silu(proj(_E_BZ)).astype(BF16)
    xq = proj(_E_XQ)
    xqg = xqg_ref[...] * ((HEAD_W ** -0.5) * LOG2E)
    for h in range(N_HEADS):
        sl = slice(h * HEAD_W, (h + 1) * HEAD_W)
        xq_ref[:, sl] = _rms_rows(xq[:, sl], xqg).astype(BF16)
    gx_ref[...] = _silu(proj(_E_XZ)).astype(BF16)


def _in_even(x2d, norm_g, w_main, w_gate, gate_bias, grp, qn, kn, xqg, tm, v_transposed_seq):
    m = x2d.shape[0]
    tm = min(tm, m)
    row = lambda width: pl.BlockSpec((tm, width), lambda i: (i, 0))
    half = jax.ShapeDtypeStruct((m, MIX_W), BF16)
    full = jax.ShapeDtypeStruct((m, MIX_W), F32)
    out_shape = [half, half, half, half, jax.ShapeDtypeStruct((m, HEAD_W), F32),
                 half, full, half, full, half, half, half, half]
    out_specs = [row(MIX_W)] * 4 + [row(HEAD_W)] + [row(MIX_W)] * 8
    if v_transposed_seq is not None:
        tiles = v_transposed_seq // tm
        out_shape[9] = jax.ShapeDtypeStruct((m // v_transposed_seq, MIX_W, v_transposed_seq), BF16)
        out_specs[9] = pl.BlockSpec((1, MIX_W, tm), lambda i: (i // tiles, 0, i % tiles))
    return pl.pallas_call(
        _in_even_kernel,
        grid=(m // tm,),
        in_specs=[
            row(D_MODEL),
            _const_spec((1, D_MODEL)),
            _const_spec((D_MODEL, _E_MAIN)),
            _const_spec((D_MODEL, HEAD_W)),
            _const_spec((1, HEAD_W)),
            _const_spec((MIX_W, MIX_W)),
            _const_spec((1, MIX_W)),
            _const_spec((1, MIX_W)),
            _const_spec((1, HEAD_W)),
        ],
        out_specs=out_specs,
        out_shape=out_shape,
        compiler_params=_params(("arbitrary",)),
        name="in_even",
    )(x2d, norm_g, w_main, w_gate, gate_bias, grp, qn, kn, xqg)


def _mlstm_kernel(q_ref, k_ref, v_ref, gt_ref, ga_ref, ng_ref, s0_ref, m0_ref, o_ref, s_ref, m_ref, *, chunk):
    c = pl.program_id(1)

    @pl.when(c == 0)
    def _():
        s_ref[...] = s0_ref[...]
        m_ref[...] = m0_ref[...]

    row = lax.broadcasted_iota(jnp.int32, (chunk, chunk), 0)
    col = lax.broadcasted_iota(jnp.int32, (chunk, chunk), 1)
    causal = col <= row
    ones = _ones_col(chunk)
    pairs = [(b, h) for b in range(q_ref.shape[0]) for h in range(N_HEADS)]
    states = {bh: s_ref[bh] for bh in pairs}
    m_prev = {bh: m_ref[bh][:, 0:1] for bh in pairs}
    results = {}
    for b in range(q_ref.shape[0]):
        gt = gt_ref[b]
        cum = jnp.dot(causal.astype(F32), gt, preferred_element_type=F32,
                      precision=lax.Precision.HIGHEST)
        for h in range(N_HEADS):
            results[b, h] = _mlstm_chain(b, h, chunk, causal, ones, gt, cum, gt.T, cum.T, states[b, h],
                                         m_prev[b, h], q_ref, k_ref, v_ref, ga_ref, ng_ref)
    for (b, h), (out, state, m_last) in results.items():
        o_ref[b, :, h * HEAD_W:(h + 1) * HEAD_W] = out
        s_ref[b, h] = state
        m_ref[b, h] = jnp.broadcast_to(m_last, (1, HEAD_W))


def _mlstm_chain(b, h, chunk, causal, ones, gt, cum, gt_t, cum_t, state, m0, q_ref, k_ref, v_ref, ga_ref, ng_ref):
    sl = slice(h * HEAD_W, (h + 1) * HEAD_W)
    b_col = cum[:, N_HEADS + h:N_HEADS + h + 1]
    i_col = gt[:, h:h + 1]
    r_row = gt_t[h:h + 1, :] - cum_t[N_HEADS + h:N_HEADS + h + 1, :]
    g_col = b_col + m0
    logw = jnp.where(causal, b_col + r_row, NEG_INF)
    m_col = jnp.maximum(g_col, jnp.max(logw, axis=-1, keepdims=True))
    w_intra = jnp.exp(logw - m_col)
    w_inter = jnp.exp(g_col - m_col)
    q = q_ref[b, :, sl]
    k = k_ref[b, :, sl]
    v_aug = jnp.concatenate([v_ref[b, :, sl], ones], axis=-1)
    sc = (_dot_nt(q, k) * w_intra).astype(BF16)
    inter = _dot(q, state.astype(BF16))
    intra = _dot(sc, v_aug)
    num = w_inter * inter[:, :HEAD_W] + intra[:, :HEAD_W]
    den = w_inter * inter[:, HEAD_W:HEAD_W + 1] + intra[:, HEAD_W:HEAD_W + 1]
    hid = num / jnp.maximum(jnp.abs(den), jnp.exp(-m_col))
    out = _rms_rows(hid, ng_ref[:, sl]) * ga_ref[b, :, sl].astype(F32)
    m_last = m_col[chunk - 1:chunk, :]
    decay = jnp.exp(g_col[chunk - 1:chunk, :] - m_last)
    w_end = jnp.exp(b_col[chunk - 1:chunk, :] - b_col + i_col - m_last)
    kv = _dot_tn(k, (v_aug.astype(F32) * w_end).astype(BF16))
    return out.astype(BF16), decay * state + kv, m_last


def _mlstm(q, k, v, gates, gate_a, norm_g, state0, m0, chunk, rows):
    bsz, seq, _ = q.shape
    chunk = min(chunk, seq)
    tok = lambda width: pl.BlockSpec((rows, chunk, width), lambda b, c: (b, c, 0))
    st_spec = pl.BlockSpec((rows, N_HEADS, HEAD_W, 2 * HEAD_W), lambda b, c: (b, 0, 0, 0))
    m_spec = pl.BlockSpec((rows, N_HEADS, 1, HEAD_W), lambda b, c: (b, 0, 0, 0))
    return pl.pallas_call(
        functools.partial(_mlstm_kernel, chunk=chunk),
        grid=(bsz // rows, seq // chunk),
        in_specs=[tok(MIX_W), tok(MIX_W), tok(MIX_W), tok(HEAD_W), tok(MIX_W),
                  _const_spec((1, MIX_W)), st_spec, m_spec],
        out_specs=[tok(MIX_W), st_spec, m_spec],
        out_shape=[jax.ShapeDtypeStruct((bsz, seq, MIX_W), BF16),
                   jax.ShapeDtypeStruct(state0.shape, F32),
                   jax.ShapeDtypeStruct(m0.shape, F32)],
        compiler_params=_params(("arbitrary", "arbitrary")),
        name="mlstm",
    )(q, k, v, gates, gate_a, norm_g, state0, m0)


def _split_maps(q):
    lane = lax.broadcasted_iota(jnp.int32, q.shape, 1)
    zero = jnp.zeros_like(q)
    return jnp.concatenate([jnp.where(lane < DK_B, q, zero), jnp.where(lane >= DK_B, q, zero)], axis=0)


def _diff_lambda(lam_refs, lam_init):
    lq1, lk1, lq2, lk2 = (r[...] for r in lam_refs)
    return (jnp.exp(jnp.sum(lq1 * lk1, axis=-1, keepdims=True))
            - jnp.exp(jnp.sum(lq2 * lk2, axis=-1, keepdims=True)) + lam_init)


def _diff_gate(o, lam_init, sg, gate):
    return (_rms_rows(o, sg) * (1.0 - lam_init) * gate.astype(F32)).astype(BF16)


def _diff_prompt_kernel(q_ref, k_ref, vt_ref, gate_ref, base_ref, dbias_ref, lq1_ref, lk1_ref, lq2_ref, lk2_ref,
                        sg_ref, o_ref, m_sc, acc_sc, *, tq, heads, lam_init):
    i = pl.program_id(2)
    m_sc[...] = jnp.full(m_sc.shape, NEG_INF, F32)
    acc_sc[...] = jnp.zeros(acc_sc.shape, F32)
    ones_rows = jnp.ones((ACC_ROWS - HEAD_W, tq), BF16)
    qq = [_split_maps(q_ref[0, :, h * HEAD_W:(h + 1) * HEAD_W]) for h in range(heads)]

    def block(j, diagonal, offset):
        start = pl.multiple_of(j * tq, tq)
        tiles_per_map = tq // HEAD_W
        scores = [_dot_nt(k_ref[0, pl.ds(start, tq), h * HEAD_W:(h + 1) * HEAD_W], qq[h]) for h in range(heads)]
        for h in range(heads):
            hs = slice(h * HEAD_W, (h + 1) * HEAD_W)
            v_aug = jnp.concatenate([vt_ref[0, hs, pl.ds(start, tq)], ones_rows], axis=0)
            m_old = m_sc[h]
            m_new, p = [], []
            for t in range(2 * tiles_per_map):
                cs = slice(t * HEAD_W, (t + 1) * HEAD_W)
                part = (t % tiles_per_map) * HEAD_W
                bias = dbias_ref[h, :, part:part + HEAD_W] if diagonal else base_ref[h]
                s = scores[h][:, cs] + bias
                m_t = jnp.maximum(m_old[:, cs], jnp.max(s, axis=0, keepdims=True) + offset[h])
                p.append(jnp.exp2(s - (m_t - offset[h])).astype(BF16))
                m_new.append(m_t)
            m_new = jnp.concatenate(m_new, axis=-1)
            acc_sc[h] = jnp.exp2(m_old - m_new) * acc_sc[h] + _dot(v_aug, jnp.concatenate(p, axis=-1))
            m_sc[h] = m_new

    def earlier(j, carry):
        blocks_back = ((j - i) * tq).astype(F32)
        block(j, False, [base_ref[h, 1:2, 0:1] * blocks_back for h in range(heads)])
        return carry

    lax.fori_loop(0, i, earlier, 0)
    block(i, True, [0.0] * heads)

    lam = _diff_lambda((lq1_ref, lk1_ref, lq2_ref, lk2_ref), lam_init)
    for h in range(heads):
        hs = slice(h * HEAD_W, (h + 1) * HEAD_W)
        acc = acc_sc[h]
        o_t = (acc[:HEAD_W, :tq] / acc[HEAD_W:HEAD_W + 1, :tq]
               - lam * (acc[:HEAD_W, tq:] / acc[HEAD_W:HEAD_W + 1, tq:]))
        o_ref[0, :, hs] = _diff_gate(o_t.T, lam_init, sg_ref[...], gate_ref[0, :, hs])


def _alibi_tables(tq):
    slopes = jnp.array(ALIBI_SLOPES, F32) * LOG2E
    kk = jnp.arange(tq, dtype=jnp.int32)[:, None]
    qi = jnp.arange(tq, dtype=jnp.int32)[None, :]
    base = slopes[:, None, None] * jnp.broadcast_to(kk.astype(F32), (tq, HEAD_W))[None]
    diag = slopes[:, None, None] * (qi - jnp.abs(qi - kk)).astype(F32)[None]
    diag = jnp.where((kk // CHUNK <= qi // CHUNK)[None], diag, NEG_INF)
    return base, diag


def _diff_prompt(q, k, vt, gate, lam_vecs, subln_g, lam_init, tq, heads):
    bsz, seq, _ = q.shape
    tq = min(tq, seq)
    base, diag = _alibi_tables(tq)
    width = heads * HEAD_W
    q_spec = pl.BlockSpec((1, tq, width), lambda b, h, i: (b, i, h))
    return pl.pallas_call(
        functools.partial(_diff_prompt_kernel, tq=tq, heads=heads, lam_init=lam_init),
        grid=(bsz, N_HEADS // heads, seq // tq),
        in_specs=[q_spec,
                  pl.BlockSpec((1, seq, width), lambda b, h, i: (b, 0, h), pipeline_mode=pl.Buffered(1)),
                  pl.BlockSpec((1, width, seq), lambda b, h, i: (b, h, 0), pipeline_mode=pl.Buffered(1)),
                  q_spec,
                  pl.BlockSpec((heads, tq, HEAD_W), lambda b, h, i: (h, 0, 0), pipeline_mode=pl.Buffered(1)),
                  pl.BlockSpec((heads, tq, tq), lambda b, h, i: (h, 0, 0), pipeline_mode=pl.Buffered(1))]
                 + [_const_spec((1, DK_B))] * 4 + [_const_spec((1, HEAD_W))],
        out_specs=q_spec,
        out_shape=jax.ShapeDtypeStruct((bsz, seq, MIX_W), BF16),
        scratch_shapes=[pltpu.VMEM((heads, 1, 2 * tq), F32), pltpu.VMEM((heads, ACC_ROWS, 2 * tq), F32)],
        compiler_params=_params(("arbitrary", "arbitrary", "arbitrary")),
        name="diff_prompt",
    )(q, k, vt, gate, base, diag, *lam_vecs, subln_g)


def _diff_sample_kernel(q_ref, k_ref, v_ref, gate_ref, slope_ref, lq1_ref, lk1_ref, lq2_ref, lk2_ref, sg_ref,
                        o_ref, *, past, lam_init):
    rows = q_ref.shape[1]
    keys = k_ref.shape[1]
    slope = slope_ref[0][:, 0:1]
    qq = _split_maps(q_ref[0])
    r = lax.broadcasted_iota(jnp.int32, (2 * rows, keys), 0)
    r = jnp.where(r >= rows, r - rows, r) + past
    c = lax.broadcasted_iota(jnp.int32, (2 * rows, keys), 1)
    s = _dot_nt(qq, k_ref[0]) - slope * jnp.abs(r - c).astype(F32)
    p = jnp.exp2(s - jnp.max(s, axis=-1, keepdims=True))
    v_aug = jnp.concatenate([v_ref[0], _ones_col(keys)], axis=-1)
    acc = _dot(p.astype(BF16), v_aug)
    lam = _diff_lambda((lq1_ref, lk1_ref, lq2_ref, lk2_ref), lam_init)
    o = (acc[:rows, :HEAD_W] / acc[:rows, HEAD_W:HEAD_W + 1]
         - lam * (acc[rows:, :HEAD_W] / acc[rows:, HEAD_W:HEAD_W + 1]))
    o_ref[0] = _diff_gate(o, lam_init, sg_ref[...], gate_ref[0])


def _diff_sample(q, k_all, v_all, gate, slopes, lam_vecs, subln_g, lam_init):
    bsz, rows, _ = q.shape
    keys = k_all.shape[1]
    q_spec = pl.BlockSpec((1, rows, HEAD_W), lambda b, h: (b, 0, h))
    kv_spec = pl.BlockSpec((1, keys, HEAD_W), lambda b, h: (b, 0, h))
    return pl.pallas_call(
        functools.partial(_diff_sample_kernel, past=keys - rows, lam_init=lam_init),
        grid=(bsz, N_HEADS),
        in_specs=[q_spec, kv_spec, kv_spec, q_spec,
                  pl.BlockSpec((1, 1, HEAD_W), lambda b, h: (h, 0, 0))]
                 + [_const_spec((1, DK_B))] * 4 + [_const_spec((1, HEAD_W))],
        out_specs=q_spec,
        out_shape=jax.ShapeDtypeStruct((bsz, rows, MIX_W), BF16),
        compiler_params=_params(("arbitrary", "arbitrary")),
        name="diff_sample",
    )(q, k_all, v_all, gate, slopes, *lam_vecs, subln_g)


def _cross_attn(xq, gate, mem_k, mem_v):
    outs = []
    for h in range(N_HEADS):
        sl = slice(h * HEAD_W, (h + 1) * HEAD_W)
        s = _dot_nt(xq[:, sl], mem_k[:, sl])
        p = jnp.exp2(s - jnp.max(s, axis=-1, keepdims=True))
        outs.append(_dot(p.astype(BF16), mem_v[:, sl]) / jnp.sum(p, axis=-1, keepdims=True))
    return (jnp.concatenate(outs, axis=-1) * gate).astype(BF16)


def _out_even_kernel(x_ref, oa_ref, ob_ref, xq_ref, gx_ref, mk_ref, mv_ref, w_ref, y_ref):
    ox = _cross_attn(xq_ref[0], gx_ref[0].astype(F32), mk_ref[0], mv_ref[0])
    mixed = jnp.concatenate([oa_ref[0], ob_ref[0], ox], axis=-1)
    y_ref[0] = x_ref[0] + _dot(mixed, w_ref[...])


def _out_even(x, oa, ob, xq, gx, mem_k, mem_v, w_out, tm):
    bsz, seq, _ = x.shape
    tm = min(tm, seq)
    tok = lambda width: pl.BlockSpec((1, tm, width), lambda b, i: (b, i, 0))
    mem_spec = pl.BlockSpec((1, N_MEM, MIX_W), lambda b, i: (b, 0, 0))
    return pl.pallas_call(
        _out_even_kernel,
        grid=(bsz, seq // tm),
        in_specs=[tok(D_MODEL), tok(MIX_W), tok(MIX_W), tok(MIX_W), tok(MIX_W), mem_spec, mem_spec,
                  _const_spec((3 * MIX_W, D_MODEL))],
        out_specs=tok(D_MODEL),
        out_shape=jax.ShapeDtypeStruct(x.shape, F32),
        compiler_params=_params(("arbitrary", "arbitrary")),
        name="out_even",
    )(x, oa, ob, xq, gx, mem_k, mem_v, w_out)


_O_CU, _O_CG, _O_CZ = 0, D_MODEL, 2 * D_MODEL
_O_XQ, _O_XZ = 3 * D_MODEL, 3 * D_MODEL + MIX_W


def _odd_kernel(x_ref, hist_ref, g_ref, w_in_ref, cw_ref, cb_ref, lg_ref, lb_ref, xqg_ref, mk_ref, mv_ref,
                w_out_ref, y_ref, tail_ref, u_sc, conv_sc, *, tm):
    i = pl.program_id(1)

    @pl.when(i == 0)
    def _():
        u_sc[0:CONV_HIST, :] = hist_ref[0]

    x = x_ref[0]
    xn = _rms_rows(x, g_ref[...]).astype(BF16)

    def proj(c0, width):
        return _dot(xn, w_in_ref[:, c0:c0 + width])

    u = proj(_O_CU, D_MODEL) * _sigmoid(proj(_O_CG, D_MODEL))
    u_sc[CONV_HIST:CONV_HIST + tm, :] = u
    rows = min(CONV_ROWS, tm)
    for c0 in range(0, D_MODEL, HEAD_W):
        cs = slice(c0, c0 + HEAD_W)
        for r0 in range(0, tm, rows):
            acc = jnp.broadcast_to(cb_ref[:, cs], (rows, HEAD_W))
            span = rows + CONV_HIST
            aligned = u_sc[r0:r0 + span, cs]
            for phase in range(SUBLANES):
                offs = [o for o in range(CONV_HIST - (CONV_W - 1), CONV_HIST + 1) if o % SUBLANES == phase]
                win = pltpu.roll(aligned, span - phase, axis=0) if phase else aligned
                for off in offs:
                    j = off - (CONV_HIST - (CONV_W - 1))
                    acc = acc + cw_ref[j:j + 1, cs] * win[off - phase:off - phase + rows]
            conv_sc[r0:r0 + rows, cs] = acc
    acc = conv_sc[...]
    mu = jnp.mean(acc, axis=-1, keepdims=True)
    cen = acc - mu
    var = jnp.mean(cen * cen, axis=-1, keepdims=True)
    c = _silu(cen * lax.rsqrt(var + EPS) * lg_ref[...] + lb_ref[...])
    c = (c * _silu(proj(_O_CZ, D_MODEL))).astype(BF16)
    xq = proj(_O_XQ, MIX_W)
    xqg = xqg_ref[...] * ((HEAD_W ** -0.5) * LOG2E)
    xq = jnp.concatenate([_rms_rows(xq[:, h * HEAD_W:(h + 1) * HEAD_W], xqg) for h in range(N_HEADS)], axis=-1)
    ox = _cross_attn(xq.astype(BF16), _silu(proj(_O_XZ, MIX_W)), mk_ref[0], mv_ref[0])
    y_ref[0] = x + _dot(jnp.concatenate([c, ox], axis=-1), w_out_ref[...])
    tail = u_sc[tm:tm + CONV_HIST, :]
    u_sc[0:CONV_HIST, :] = tail
    tail_ref[0] = tail


def _odd_layer(x, hist, norm_g, w_in, conv_w, conv_b, ln_g, ln_b, xqg, mem_k, mem_v, w_out, tm):
    bsz, seq, _ = x.shape
    tm = min(tm, seq)
    tok = pl.BlockSpec((1, tm, D_MODEL), lambda b, i: (b, i, 0))
    hist_spec = pl.BlockSpec((1, CONV_HIST, D_MODEL), lambda b, i: (b, 0, 0))
    mem_spec = pl.BlockSpec((1, N_MEM, MIX_W), lambda b, i: (b, 0, 0))
    vec = _const_spec((1, D_MODEL))
    return pl.pallas_call(
        functools.partial(_odd_kernel, tm=tm),
        grid=(bsz, seq // tm),
        in_specs=[tok, hist_spec, vec, _const_spec(w_in.shape), _const_spec(conv_w.shape), vec, vec, vec,
                  _const_spec((1, HEAD_W)), mem_spec, mem_spec, _const_spec(w_out.shape)],
        out_specs=[tok, hist_spec],
        out_shape=[jax.ShapeDtypeStruct(x.shape, F32), jax.ShapeDtypeStruct((bsz, CONV_HIST, D_MODEL), F32)],
        scratch_shapes=[pltpu.VMEM((CONV_HIST + tm, D_MODEL), F32), pltpu.VMEM((tm, D_MODEL), F32)],
        compiler_params=_params(("arbitrary", "arbitrary")),
        name="odd_layer",
    )(x, hist, norm_g, w_in, conv_w, conv_b, ln_g, ln_b, xqg, mem_k, mem_v, w_out)


def _even_layer(x, mem_k, mem_v, hist, wts, lam_init, tm_in, tm_out, chunk, tq):
    (norm_g, w_main, w_gate, gate_bias, grp, qn, kn, xqg, mlstm_g, slopes, lam_vecs, subln_g, w_out) = wts
    bsz, seq, _ = x.shape
    (qa, ka, va, ga, gates, qb, kb, kbh, vb, vbh, gb, xq, gx) = _in_even(
        x.reshape(bsz * seq, D_MODEL), norm_g, w_main, w_gate, gate_bias, grp, qn, kn, xqg, tm_in,
        v_transposed_seq=seq if hist is None else None)
    tok = lambda a: a.reshape(bsz, seq, a.shape[-1])
    if hist is None:
        state0 = jnp.zeros((bsz, N_HEADS, HEAD_W, 2 * HEAD_W), F32)
        m0 = jnp.zeros((bsz, N_HEADS, 1, HEAD_W), F32)
    else:
        k_past, v_past, c0, n0, m0 = hist
        state0 = jnp.concatenate([c0, n0[..., None], jnp.zeros(c0.shape[:-1] + (HEAD_W - 1,), F32)], axis=-1)
        m0 = jnp.broadcast_to(m0[..., None, None], (bsz, N_HEADS, 1, HEAD_W))
    oa, state1, m1 = _mlstm(tok(qa), tok(ka), tok(va), tok(gates), tok(ga), mlstm_g, state0, m0, chunk,
                            rows=min(bsz, MLSTM_ROWS))
    if hist is None:
        ob = _diff_prompt(tok(qb), tok(kbh), vbh, tok(gb), lam_vecs, subln_g, lam_init, tq, heads=2)
    else:
        past = k_past.shape[1]
        k_all = jnp.concatenate([k_past.reshape(bsz, past, MIX_W).astype(BF16), tok(kbh)], axis=1)
        v_all = jnp.concatenate([v_past.reshape(bsz, past, MIX_W).astype(BF16), tok(vbh)], axis=1)
        ob = _diff_sample(tok(qb), k_all, v_all, tok(gb), slopes, lam_vecs, subln_g, lam_init)
    y = _out_even(x, oa, ob, tok(xq), tok(gx), mem_k, mem_v, w_out, tm_out)
    return (y, kb.reshape(bsz, seq, N_HEADS, HEAD_W), vb.reshape(bsz, seq, N_HEADS, HEAD_W),
            state1[..., :HEAD_W], state1[..., HEAD_W], m1[:, :, 0, 0])


def kernel(x_prompt, x_sample, mem_prompt, cache_xk, cache_xv, cache_k, cache_v, state_C, state_n, state_m,
           state_conv, norm_g, w_in_a, b_ig, b_fg, mlstm_norm_g, qn_g, kn_g, lam_q1, lam_k1, lam_q2, lam_k2,
           subln_g, w_out_a, w_in_c, conv_w, conv_b, conv_ln_g, conv_ln_b, w_out_c, mem_norm_g, w_mem_kv,
           xq_norm_g, xk_norm_g):
    depth = norm_g.shape[0]
    bsz = x_prompt.shape[0]
    dec = x_sample.shape[0]
    p_xk, p_xv = _mem_kv(mem_prompt, mem_norm_g, w_mem_kv, xk_norm_g)
    mem_k_p, mem_v_p = p_xk.astype(BF16), p_xv.astype(BF16)
    mem_k_s = cache_xk.reshape(depth, dec, N_MEM, MIX_W).astype(BF16)
    mem_v_s = cache_xv.reshape(depth, dec, N_MEM, MIX_W).astype(BF16)

    lane_grp = jnp.arange(MIX_W) // DK_B
    grp = (lane_grp[:, None] == lane_grp[None, :]).astype(BF16)
    slopes = jnp.broadcast_to(jnp.array(ALIBI_SLOPES, F32)[:, None, None] * LOG2E, (N_HEADS, 1, HEAD_W))
    n_gate = 2 * N_HEADS
    gate0 = 5 * MIX_W

    yp, ys = x_prompt, x_sample
    outs = {name: [] for name in ("p_k", "p_v", "p_C", "p_n", "p_m", "p_conv",
                                  "s_k", "s_v", "s_C", "s_n", "s_m", "s_conv")}
    for layer in range(depth):
        if layer % 2 == 0:
            e = layer // 2
            w = w_in_a[e]
            w_main = jnp.concatenate([w[:, :gate0], w[:, gate0 + n_gate:]], axis=1).astype(BF16)
            w_gate = jnp.pad(w[:, gate0:gate0 + n_gate], ((0, 0), (0, HEAD_W - n_gate))).astype(BF16)
            gate_bias = jnp.pad(jnp.concatenate([b_ig[e], b_fg[e]]), (0, HEAD_W - n_gate)).reshape(1, HEAD_W)
            wts = (norm_g[layer].reshape(1, D_MODEL), w_main, w_gate, gate_bias, grp,
                   jnp.tile(qn_g[e], MIX_W // DK_B).reshape(1, MIX_W),
                   jnp.tile(kn_g[e], MIX_W // DK_B).reshape(1, MIX_W),
                   xq_norm_g[layer].reshape(1, HEAD_W), mlstm_norm_g[e].reshape(1, MIX_W), slopes,
                   tuple(v[e].reshape(1, DK_B) for v in (lam_q1, lam_k1, lam_q2, lam_k2)),
                   subln_g[e].reshape(1, HEAD_W), w_out_a[e].astype(BF16))
            lam_init = _lambda_init(layer)
            yp, k_new, v_new, c1, n1, m1 = _even_layer(yp, mem_k_p[layer], mem_v_p[layer], None, wts, lam_init,
                                                       tm_in=256, tm_out=512, chunk=256, tq=512)
            for name, val in zip(("p_k", "p_v", "p_C", "p_n", "p_m"), (k_new, v_new, c1, n1, m1)):
                outs[name].append(val)
            hist = (cache_k[e], cache_v[e], state_C[e], state_n[e], state_m[e])
            ys, k_new, v_new, c1, n1, m1 = _even_layer(ys, mem_k_s[layer], mem_v_s[layer], hist, wts, lam_init,
                                                       tm_in=256, tm_out=512, chunk=256, tq=512)
            for name, val in zip(("s_k", "s_v", "s_C", "s_n", "s_m"), (k_new, v_new, c1, n1, m1)):
                outs[name].append(val)
        else:
            o = layer // 2
            vec = lambda a: a.reshape(1, -1)
            wts = (vec(norm_g[layer]), w_in_c[o].astype(BF16), jnp.pad(conv_w[o], ((0, 1), (0, 0))),
                   vec(conv_b[o]), vec(conv_ln_g[o]), vec(conv_ln_b[o]), vec(xq_norm_g[layer]))
            w_out = w_out_c[o].astype(BF16)
            pad = CONV_HIST - (CONV_W - 1)
            zero_hist = jnp.zeros((bsz, CONV_HIST, D_MODEL), F32)
            yp, tail = _odd_layer(yp, zero_hist, *wts, mem_k_p[layer], mem_v_p[layer], w_out, tm=256)
            outs["p_conv"].append(tail[:, pad:])
            hist = jnp.pad(state_conv[o], ((0, 0), (pad, 0), (0, 0)))
            ys, tail = _odd_layer(ys, hist, *wts, mem_k_s[layer], mem_v_s[layer], w_out, tm=256)
            outs["s_conv"].append(tail[:, pad:])

    head5 = lambda a: a.reshape(a.shape[:-1] + (N_HEADS, HEAD_W))
    st = {name: jnp.stack(vals) for name, vals in outs.items()}
    return (yp, ys, head5(p_xk), head5(p_xv), st["p_k"], st["p_v"], st["p_C"], st["p_n"], st["p_m"], st["p_conv"],
            st["s_k"], st["s_v"], st["s_C"], st["s_n"], st["s_m"], st["s_conv"])
```

```python
import functools
import math

import jax
import jax.numpy as jnp
from jax import lax
from jax.experimental import pallas as pl
from jax.experimental.pallas import tpu as pltpu

F32 = jnp.float32
BF16 = jnp.bfloat16

D_MODEL = 1024
CHUNK = 64
EPS = 1e-6
N_HEADS = 4
HEAD_W = 128
MIX_W = N_HEADS * HEAD_W
DK_B = 64
CONV_W = 31
CONV_HIST = 32
CONV_ROWS = 64
SUBLANES = 8
MLSTM_ROWS = 1
N_MEM = 256
ALIBI_SLOPES = tuple(2.0 ** (-8.0 * (h + 1) / N_HEADS) for h in range(N_HEADS))
VMEM_LIMIT = 56 * 1024 * 1024

NEG_INF = float("-inf")
LOG2E = math.log2(math.e)
ACC_ROWS = HEAD_W + 16


def _lambda_init(layer):
    return 0.8 - 0.6 * math.exp(-0.3 * layer)


def _params(sem):
    return pltpu.CompilerParams(dimension_semantics=sem, vmem_limit_bytes=VMEM_LIMIT)


def _const_spec(shape):
    zeros = (0,) * len(shape)
    return pl.BlockSpec(shape, lambda *_: zeros)


def _rms_rows(x, g):
    return x * lax.rsqrt(jnp.mean(x * x, axis=-1, keepdims=True) + EPS) * g


def _sigmoid(x):
    return 1.0 / (1.0 + jnp.exp2(x * (-LOG2E)))


def _silu(x):
    return x * _sigmoid(x)


def _dot(a, b):
    return jnp.dot(a, b, preferred_element_type=F32)


def _dot_nt(a, b):
    return lax.dot_general(a, b, (((1,), (1,)), ((), ())), preferred_element_type=F32)


def _dot_tn(a, b):
    return lax.dot_general(a, b, (((0,), (0,)), ((), ())), preferred_element_type=F32)


def _ones_col(rows):
    lane = lax.broadcasted_iota(jnp.int32, (rows, HEAD_W), 1)
    return jnp.where(lane == 0, 1.0, 0.0).astype(BF16)


def _mem_kv_kernel(mem_ref, g_ref, w_ref, kg_ref, k_ref, v_ref):
    xn = _rms_rows(mem_ref[0], g_ref[0]).astype(BF16)
    kv = _dot(xn, w_ref[0])
    kg = kg_ref[0]
    for h in range(N_HEADS):
        sl = slice(h * HEAD_W, (h + 1) * HEAD_W)
        k_ref[0, 0, :, sl] = _rms_rows(kv[:, sl], kg)
    v_ref[0, 0] = kv[:, MIX_W:]


def _mem_kv(mem, g, w_kv, kg):
    depth = w_kv.shape[0]
    bsz, n_mem, _ = mem.shape
    out = jax.ShapeDtypeStruct((depth, bsz, n_mem, MIX_W), F32)
    return pl.pallas_call(
        _mem_kv_kernel,
        grid=(depth, bsz),
        in_specs=[
            pl.BlockSpec((1, n_mem, D_MODEL), lambda l, b: (b, 0, 0)),
            pl.BlockSpec((1, 1, D_MODEL), lambda l, b: (l, 0, 0)),
            pl.BlockSpec((1, D_MODEL, 2 * MIX_W), lambda l, b: (l, 0, 0)),
            pl.BlockSpec((1, 1, HEAD_W), lambda l, b: (l, 0, 0)),
        ],
        out_specs=[pl.BlockSpec((1, 1, n_mem, MIX_W), lambda l, b: (l, b, 0, 0))] * 2,
        out_shape=[out, out],
        compiler_params=_params(("arbitrary", "arbitrary")),
        name="mem_kv",
    )(mem, g.reshape(depth, 1, D_MODEL), w_kv.astype(BF16), kg.reshape(depth, 1, HEAD_W))


_E_AQ, _E_AK, _E_AV, _E_AO, _E_AZ, _E_BQ, _E_BK, _E_BV, _E_BZ, _E_XQ, _E_XZ = (i * MIX_W for i in range(11))
_E_MAIN = 11 * MIX_W


def _group_mean_sq(y, grp):
    sq = y * y
    hi = sq.astype(BF16)
    lo = (sq - hi.astype(F32)).astype(BF16)
    return (_dot(hi, grp) + _dot(lo, grp)) * (1.0 / DK_B)


def _store_head_rows(ref, x):
    tokens = x.shape[0]
    for h in range(N_HEADS):
        ref[pl.ds(h, tokens, stride=N_HEADS), :] = x[:, h * HEAD_W:(h + 1) * HEAD_W]


def _in_even_kernel(x_ref, g_ref, w_ref, wg_ref, gbias_ref, grp_ref, qn_ref, kn_ref, xqg_ref,
                    qa_ref, ka_ref, va_ref, ga_ref, gt_ref, qb_ref, kb_ref, kbh_ref, vb_ref, vbh_ref,
                    gb_ref, xq_ref, gx_ref):
    xn = _rms_rows(x_ref[...], g_ref[...]).astype(BF16)

    def proj(c0):
        return _dot(xn, w_ref[:, c0:c0 + MIX_W])

    qa_ref[...] = proj(_E_AQ).astype(BF16)
    ka_ref[...] = (proj(_E_AK) * (HEAD_W ** -0.5)).astype(BF16)
    va_ref[...] = proj(_E_AV).astype(BF16)
    ga_ref[...] = (_sigmoid(proj(_E_AO)) * _silu(proj(_E_AZ))).astype(BF16)
    gt = _dot(xn, wg_ref[...]) + gbias_ref[...]
    lane = lax.broadcasted_iota(jnp.int32, gt.shape, 1)
    log_sig = jnp.minimum(gt, 0.0) - jnp.log1p(jnp.exp(-jnp.abs(gt)))
    gt_ref[...] = jnp.where((lane >= N_HEADS) & (lane < 2 * N_HEADS), log_sig, gt)
    grp = grp_ref[...]
    bq = proj(_E_BQ)
    q_scale = (DK_B ** -0.5) * LOG2E
    qb_ref[...] = (bq * lax.rsqrt(_group_mean_sq(bq, grp) + EPS) * qn_ref[...] * q_scale).astype(BF16)
    bk = proj(_E_BK)
    kb = bk * lax.rsqrt(_group_mean_sq(bk, grp) + EPS) * kn_ref[...]
    _store_head_rows(kb_ref, kb)
    kbh_ref[...] = kb.astype(BF16)
    bv = proj(_E_BV)
    _store_head_rows(vb_ref, bv)
    if len(vbh_ref.shape) == 3:
        vbh_ref[0] = bv.T.astype(BF16)
    else:
        vbh_ref[...] = bv.astype(BF16)
    gb_ref[...] = _silu(proj(_E_BZ)).astype(BF16)
    xq = proj(_E_XQ)
    xqg = xqg_ref[...] * ((HEAD_W ** -0.5) * LOG2E)
    for h in range(N_HEADS):
        sl = slice(h * HEAD_W, (h + 1) * HEAD_W)
        xq_ref[:, sl] = _rms_rows(xq[:, sl], xqg).astype(BF16)
    gx_ref[...] = _silu(proj(_E_XZ)).astype(BF16)


def _in_even(x2d, norm_g, w_main, w_gate, gate_bias, grp, qn, kn, xqg, tm, v_transposed_seq):
    m = x2d.shape[0]
    tm = min(tm, m)
    row = lambda width: pl.BlockSpec((tm, width), lambda i: (i, 0))
    half = jax.ShapeDtypeStruct((m, MIX_W), BF16)
    full = jax.ShapeDtypeStruct((m * N_HEADS, HEAD_W), F32)
    head_rows = pl.BlockSpec((tm * N_HEADS, HEAD_W), lambda i: (i, 0))
    out_shape = [half, half, half, half, jax.ShapeDtypeStruct((m, HEAD_W), F32),
                 half, full, half, full, half, half, half, half]
    out_specs = [row(MIX_W)] * 4 + [row(HEAD_W)] + [row(MIX_W)] * 8
    out_specs[6] = out_specs[8] = head_rows
    if v_transposed_seq is not None:
        tiles = v_transposed_seq // tm
        out_shape[9] = jax.ShapeDtypeStruct((m // v_transposed_seq, MIX_W, v_transposed_seq), BF16)
        out_specs[9] = pl.BlockSpec((1, MIX_W, tm), lambda i: (i // tiles, 0, i % tiles))
    return pl.pallas_call(
        _in_even_kernel,
        grid=(m // tm,),
        in_specs=[
            row(D_MODEL),
            _const_spec((1, D_MODEL)),
            _const_spec((D_MODEL, _E_MAIN)),
            _const_spec((D_MODEL, HEAD_W)),
            _const_spec((1, HEAD_W)),
            _const_spec((MIX_W, MIX_W)),
            _const_spec((1, MIX_W)),
            _const_spec((1, MIX_W)),
            _const_spec((1, HEAD_W)),
        ],
        out_specs=out_specs,
        out_shape=out_shape,
        compiler_params=_params(("arbitrary",)),
        name="in_even",
    )(x2d, norm_g, w_main, w_gate, gate_bias, grp, qn, kn, xqg)


def _mlstm_kernel(q_ref, k_ref, v_ref, gt_ref, ga_ref, ng_ref, s0_ref, m0_ref, o_ref, s_ref, m_ref, *, chunk):
    c = pl.program_id(1)

    @pl.when(c == 0)
    def _():
        s_ref[...] = s0_ref[...]
        m_ref[...] = m0_ref[...]

    row = lax.broadcasted_iota(jnp.int32, (chunk, chunk), 0)
    col = lax.broadcasted_iota(jnp.int32, (chunk, chunk), 1)
    causal = col <= row
    ones = _ones_col(chunk)
    pairs = [(b, h) for b in range(q_ref.shape[0]) for h in range(N_HEADS)]
    states = {bh: s_ref[bh] for bh in pairs}
    m_prev = {bh: m_ref[bh][:, 0:1] for bh in pairs}
    results = {}
    for b in range(q_ref.shape[0]):
        gt = gt_ref[b]
        cum = jnp.dot(causal.astype(F32), gt, preferred_element_type=F32,
                      precision=lax.Precision.HIGHEST)
        for h in range(N_HEADS):
            results[b, h] = _mlstm_chain(b, h, chunk, causal, ones, gt, cum, gt.T, cum.T, states[b, h],
                                         m_prev[b, h], q_ref, k_ref, v_ref, ga_ref, ng_ref)
    for (b, h), (out, state, m_last) in results.items():
        o_ref[b, :, h * HEAD_W:(h + 1) * HEAD_W] = out
        s_ref[b, h] = state
        m_ref[b, h] = jnp.broadcast_to(m_last, (1, HEAD_W))


def _mlstm_chain(b, h, chunk, causal, ones, gt, cum, gt_t, cum_t, state, m0, q_ref, k_ref, v_ref, ga_ref, ng_ref):
    sl = slice(h * HEAD_W, (h + 1) * HEAD_W)
    b_col = cum[:, N_HEADS + h:N_HEADS + h + 1]
    i_col = gt[:, h:h + 1]
    r_row = gt_t[h:h + 1, :] - cum_t[N_HEADS + h:N_HEADS + h + 1, :]
    g_col = b_col + m0
    logw = jnp.where(causal, b_col + r_row, NEG_INF)
    m_col = jnp.maximum(g_col, jnp.max(logw, axis=-1, keepdims=True))
    w_intra = jnp.exp(logw - m_col)
    w_inter = jnp.exp(g_col - m_col)
    q = q_ref[b, :, sl]
    k = k_ref[b, :, sl]
    v_aug = jnp.concatenate([v_ref[b, :, sl], ones], axis=-1)
    sc = (_dot_nt(q, k) * w_intra).astype(BF16)
    inter = _dot(q, state.astype(BF16))
    intra = _dot(sc, v_aug)
    num = w_inter * inter[:, :HEAD_W] + intra[:, :HEAD_W]
    den = w_inter * inter[:, HEAD_W:HEAD_W + 1] + intra[:, HEAD_W:HEAD_W + 1]
    hid = num / jnp.maximum(jnp.abs(den), jnp.exp(-m_col))
    out = _rms_rows(hid, ng_ref[:, sl]) * ga_ref[b, :, sl].astype(F32)
    m_last = m_col[chunk - 1:chunk, :]
    decay = jnp.exp(g_col[chunk - 1:chunk, :] - m_last)
    w_end = jnp.exp(b_col[chunk - 1:chunk, :] - b_col + i_col - m_last)
    kv = _dot_tn(k, (v_aug.astype(F32) * w_end).astype(BF16))
    return out.astype(BF16), decay * state + kv, m_last


def _mlstm(q, k, v, gates, gate_a, norm_g, state0, m0, chunk, rows):
    bsz, seq, _ = q.shape
    chunk = min(chunk, seq)
    tok = lambda width: pl.BlockSpec((rows, chunk, width), lambda b, c: (b, c, 0))
    st_spec = pl.BlockSpec((rows, N_HEADS, HEAD_W, 2 * HEAD_W), lambda b, c: (b, 0, 0, 0))
    m_spec = pl.BlockSpec((rows, N_HEADS, 1, HEAD_W), lambda b, c: (b, 0, 0, 0))
    return pl.pallas_call(
        functools.partial(_mlstm_kernel, chunk=chunk),
        grid=(bsz // rows, seq // chunk),
        in_specs=[tok(MIX_W), tok(MIX_W), tok(MIX_W), tok(HEAD_W), tok(MIX_W),
                  _const_spec((1, MIX_W)), st_spec, m_spec],
        out_specs=[tok(MIX_W), st_spec, m_spec],
        out_shape=[jax.ShapeDtypeStruct((bsz, seq, MIX_W), BF16),
                   jax.ShapeDtypeStruct(state0.shape, F32),
                   jax.ShapeDtypeStruct(m0.shape, F32)],
        compiler_params=_params(("arbitrary", "arbitrary")),
        name="mlstm",
    )(q, k, v, gates, gate_a, norm_g, state0, m0)


def _split_maps(q):
    lane = lax.broadcasted_iota(jnp.int32, q.shape, 1)
    zero = jnp.zeros_like(q)
    return jnp.concatenate([jnp.where(lane < DK_B, q, zero), jnp.where(lane >= DK_B, q, zero)], axis=0)


def _diff_lambda(lam_refs, lam_init):
    lq1, lk1, lq2, lk2 = (r[...] for r in lam_refs)
    return (jnp.exp(jnp.sum(lq1 * lk1, axis=-1, keepdims=True))
            - jnp.exp(jnp.sum(lq2 * lk2, axis=-1, keepdims=True)) + lam_init)


def _diff_gate(o, lam_init, sg, gate):
    return (_rms_rows(o, sg) * (1.0 - lam_init) * gate.astype(F32)).astype(BF16)


def _diff_prompt_kernel(q_ref, k_ref, vt_ref, gate_ref, bias_ref, lq1_ref, lk1_ref, lq2_ref, lk2_ref,
                        sg_ref, o_ref, m_sc, acc_sc, s_a, s_b, mb_a, mb_b, *, tq, heads, lam_init):
    i = pl.program_id(2)
    tiles_per_map = tq // HEAD_W
    m_sc[...] = jnp.full(m_sc.shape, NEG_INF, F32)
    acc_sc[...] = jnp.zeros(acc_sc.shape, F32)
    ones_rows = jnp.ones((ACC_ROWS - HEAD_W, tq), BF16)
    qq = [_split_maps(q_ref[0, :, h * HEAD_W:(h + 1) * HEAD_W]) for h in range(heads)]

    def scores(h, blk, s_dst, mb_dst):
        start = pl.multiple_of(blk * tq, tq)
        table = (blk == i).astype(jnp.int32)
        s = _dot_nt(k_ref[0, pl.ds(start, tq), h * HEAD_W:(h + 1) * HEAD_W], qq[h])
        block_max = []
        for t in range(2 * tiles_per_map):
            cs = slice(t * HEAD_W, (t + 1) * HEAD_W)
            part = (t % tiles_per_map) * HEAD_W
            s_t = s[:, cs] + bias_ref[h, table, :, part:part + HEAD_W]
            s_dst[h, :, cs] = s_t
            block_max.append(jnp.max(s_t, axis=0, keepdims=True))
        mb_dst[h] = jnp.concatenate(block_max, axis=-1)

    def absorb(h, blk, s_src, mb_src):
        start = pl.multiple_of(blk * tq, tq)
        hs = slice(h * HEAD_W, (h + 1) * HEAD_W)
        offset = bias_ref[h, 0, 1:2, 0:1] * ((blk - i) * tq).astype(F32)
        m_old = m_sc[h]
        m_new = jnp.maximum(m_old, mb_src[h] + offset)
        shift = m_new - offset
        p = [jnp.exp2(s_src[h, :, t * HEAD_W:(t + 1) * HEAD_W] - shift[:, t * HEAD_W:(t + 1) * HEAD_W]).astype(BF16)
             for t in range(2 * tiles_per_map)]
        v_aug = jnp.concatenate([vt_ref[0, hs, pl.ds(start, tq)], ones_rows], axis=0)
        acc_sc[h] = jnp.exp2(m_old - m_new) * acc_sc[h] + _dot(v_aug, jnp.concatenate(p, axis=-1))
        m_sc[h] = m_new

    buffers = ((s_a, mb_a), (s_b, mb_b))
    for h in range(heads):
        scores(h, 0, *buffers[0])

    def step(j, carry):
        for parity in (0, 1):
            @pl.when(j % 2 == parity)
            def _():
                for h in range(heads):
                    scores(h, j + 1, *buffers[1 - parity])
                    absorb(h, j, *buffers[parity])
        return carry

    lax.fori_loop(0, i, step, 0)
    for parity in (0, 1):
        @pl.when(i % 2 == parity)
        def _():
            for h in range(heads):
                absorb(h, i, *buffers[parity])

    lam = _diff_lambda((lq1_ref, lk1_ref, lq2_ref, lk2_ref), lam_init)
    for h in range(heads):
        hs = slice(h * HEAD_W, (h + 1) * HEAD_W)
        acc = acc_sc[h]
        o_t = (acc[:HEAD_W, :tq] / acc[HEAD_W:HEAD_W + 1, :tq]
               - lam * (acc[:HEAD_W, tq:] / acc[HEAD_W:HEAD_W + 1, tq:]))
        o_ref[0, :, hs] = _diff_gate(o_t.T, lam_init, sg_ref[...], gate_ref[0, :, hs])


def _alibi_tables(tq):
    slopes = jnp.array(ALIBI_SLOPES, F32) * LOG2E
    kk = jnp.arange(tq, dtype=jnp.int32)[:, None]
    qi = jnp.arange(tq, dtype=jnp.int32)[None, :]
    base = slopes[:, None, None] * jnp.broadcast_to(kk.astype(F32), (tq, tq))[None]
    diag = slopes[:, None, None] * (qi - jnp.abs(qi - kk)).astype(F32)[None]
    diag = jnp.where((kk // CHUNK <= qi // CHUNK)[None], diag, NEG_INF)
    return jnp.stack([base, diag], axis=1)


def _diff_prompt(q, k, vt, gate, lam_vecs, subln_g, lam_init, tq, heads):
    bsz, seq, _ = q.shape
    tq = min(tq, seq)
    bias = _alibi_tables(tq)
    width = heads * HEAD_W
    score_buf = pltpu.VMEM((heads, tq, 2 * tq), F32)
    stat_buf = pltpu.VMEM((heads, 1, 2 * tq), F32)
    q_spec = pl.BlockSpec((1, tq, width), lambda b, h, i: (b, i, h))
    return pl.pallas_call(
        functools.partial(_diff_prompt_kernel, tq=tq, heads=heads, lam_init=lam_init),
        grid=(bsz, N_HEADS // heads, seq // tq),
        in_specs=[q_spec,
                  pl.BlockSpec((1, seq, width), lambda b, h, i: (b, 0, h), pipeline_mode=pl.Buffered(1)),
                  pl.BlockSpec((1, width, seq), lambda b, h, i: (b, h, 0), pipeline_mode=pl.Buffered(1)),
                  q_spec,
                  pl.BlockSpec((heads, 2, tq, tq), lambda b, h, i: (h, 0, 0, 0), pipeline_mode=pl.Buffered(1))]
                 + [_const_spec((1, DK_B))] * 4 + [_const_spec((1, HEAD_W))],
        out_specs=q_spec,
        out_shape=jax.ShapeDtypeStruct((bsz, seq, MIX_W), BF16),
        scratch_shapes=[stat_buf, pltpu.VMEM((heads, ACC_ROWS, 2 * tq), F32),
                        score_buf, score_buf, stat_buf, stat_buf],
        compiler_params=_params(("arbitrary", "arbitrary", "arbitrary")),
        name="diff_prompt",
    )(q, k, vt, gate, bias, *lam_vecs, subln_g)


def _diff_sample_kernel(q_ref, k_ref, v_ref, kp_ref, vp_ref, gate_ref, slope_ref, lq1_ref, lk1_ref, lq2_ref,
                        lk2_ref, sg_ref, o_ref, *, lam_init):
    rows = q_ref.shape[1]
    past = kp_ref.shape[1] // N_HEADS
    lam = _diff_lambda((lq1_ref, lk1_ref, lq2_ref, lk2_ref), lam_init)
    r = lax.broadcasted_iota(jnp.int32, (2 * rows, 1), 0)
    r = jnp.where(r >= rows, r - rows, r)
    dist_past = (past + r - lax.broadcasted_iota(jnp.int32, (2 * rows, past), 1)).astype(F32)
    dist_new = jnp.abs(r - lax.broadcasted_iota(jnp.int32, (2 * rows, rows), 1)).astype(F32)
    for h in range(N_HEADS):
        hs = slice(h * HEAD_W, (h + 1) * HEAD_W)
        slope = slope_ref[h][:, 0:1]
        qq = _split_maps(q_ref[0, :, hs])
        k_past = kp_ref[0, pl.ds(h, past, stride=N_HEADS), :].astype(BF16)
        v_past = vp_ref[0, pl.ds(h, past, stride=N_HEADS), :].astype(BF16)
        s_past = _dot_nt(qq, k_past) - slope * dist_past
        s_new = _dot_nt(qq, k_ref[0, :, hs]) - slope * dist_new
        m = jnp.maximum(jnp.max(s_past, axis=-1, keepdims=True), jnp.max(s_new, axis=-1, keepdims=True))
        acc = (_dot(jnp.exp2(s_past - m).astype(BF16), jnp.concatenate([v_past, _ones_col(past)], axis=-1))
               + _dot(jnp.exp2(s_new - m).astype(BF16), jnp.concatenate([v_ref[0, :, hs], _ones_col(rows)], axis=-1)))
        o = (acc[:rows, :HEAD_W] / acc[:rows, HEAD_W:HEAD_W + 1]
             - lam * (acc[rows:, :HEAD_W] / acc[rows:, HEAD_W:HEAD_W + 1]))
        o_ref[0, :, hs] = _diff_gate(o, lam_init, sg_ref[...], gate_ref[0, :, hs])


def _diff_sample(q, k_new, v_new, k_past, v_past, gate, slopes, lam_vecs, subln_g, lam_init):
    bsz, rows, _ = q.shape
    past = k_past.shape[1]
    tok = pl.BlockSpec((1, rows, MIX_W), lambda b: (b, 0, 0))
    cache = pl.BlockSpec((1, past * N_HEADS, HEAD_W), lambda b: (b, 0, 0))
    head_rows = lambda a: a.reshape(bsz, past * N_HEADS, HEAD_W)
    return pl.pallas_call(
        functools.partial(_diff_sample_kernel, lam_init=lam_init),
        grid=(bsz,),
        in_specs=[tok, tok, tok, cache, cache, tok, _const_spec((N_HEADS, 1, HEAD_W))]
                 + [_const_spec((1, DK_B))] * 4 + [_const_spec((1, HEAD_W))],
        out_specs=tok,
        out_shape=jax.ShapeDtypeStruct((bsz, rows, MIX_W), BF16),
        compiler_params=_params(("arbitrary",)),
        name="diff_sample",
    )(q, k_new, v_new, head_rows(k_past), head_rows(v_past), gate, slopes, *lam_vecs, subln_g)


def _cross_attn(xq, gate, mem_k, mem_v):
    outs = []
    for h in range(N_HEADS):
        sl = slice(h * HEAD_W, (h + 1) * HEAD_W)
        s = _dot_nt(xq[:, sl], mem_k[:, sl])
        p = jnp.exp2(s - jnp.max(s, axis=-1, keepdims=True))
        outs.append(_dot(p.astype(BF16), mem_v[:, sl]) / jnp.sum(p, axis=-1, keepdims=True))
    return (jnp.concatenate(outs, axis=-1) * gate).astype(BF16)


def _out_even_kernel(x_ref, oa_ref, ob_ref, xq_ref, gx_ref, mk_ref, mv_ref, w_ref, y_ref):
    ox = _cross_attn(xq_ref[0], gx_ref[0].astype(F32), mk_ref[0], mv_ref[0])
    mixed = jnp.concatenate([oa_ref[0], ob_ref[0], ox], axis=-1)
    y_ref[0] = x_ref[0] + _dot(mixed, w_ref[...])


def _out_even(x, oa, ob, xq, gx, mem_k, mem_v, w_out, tm):
    bsz, seq, _ = x.shape
    tm = min(tm, seq)
    tok = lambda width: pl.BlockSpec((1, tm, width), lambda b, i: (b, i, 0))
    mem_spec = pl.BlockSpec((1, N_MEM, MIX_W), lambda b, i: (b, 0, 0))
    return pl.pallas_call(
        _out_even_kernel,
        grid=(bsz, seq // tm),
        in_specs=[tok(D_MODEL), tok(MIX_W), tok(MIX_W), tok(MIX_W), tok(MIX_W), mem_spec, mem_spec,
                  _const_spec((3 * MIX_W, D_MODEL))],
        out_specs=tok(D_MODEL),
        out_shape=jax.ShapeDtypeStruct(x.shape, F32),
        compiler_params=_params(("arbitrary", "arbitrary")),
        name="out_even",
    )(x, oa, ob, xq, gx, mem_k, mem_v, w_out)


_O_CU, _O_CG, _O_CZ = 0, D_MODEL, 2 * D_MODEL
_O_XQ, _O_XZ = 3 * D_MODEL, 3 * D_MODEL + MIX_W


def _odd_kernel(x_ref, hist_ref, g_ref, w_in_ref, cw_ref, cb_ref, lg_ref, lb_ref, xqg_ref, mk_ref, mv_ref,
                w_out_ref, y_ref, tail_ref, u_sc, conv_sc, *, tm):
    i = pl.program_id(1)

    @pl.when(i == 0)
    def _():
        u_sc[0:CONV_HIST, :] = hist_ref[0]

    x = x_ref[0]
    xn = _rms_rows(x, g_ref[...]).astype(BF16)

    def proj(c0, width):
        return _dot(xn, w_in_ref[:, c0:c0 + width])

    u = proj(_O_CU, D_MODEL) * _sigmoid(proj(_O_CG, D_MODEL))
    u_sc[CONV_HIST:CONV_HIST + tm, :] = u
    rows = min(CONV_ROWS, tm)
    for c0 in range(0, D_MODEL, HEAD_W):
        cs = slice(c0, c0 + HEAD_W)
        for r0 in range(0, tm, rows):
            acc = jnp.broadcast_to(cb_ref[:, cs], (rows, HEAD_W))
            span = rows + CONV_HIST
            aligned = u_sc[r0:r0 + span, cs]
            for phase in range(SUBLANES):
                offs = [o for o in range(CONV_HIST - (CONV_W - 1), CONV_HIST + 1) if o % SUBLANES == phase]
                win = pltpu.roll(aligned, span - phase, axis=0) if phase else aligned
                for off in offs:
                    j = off - (CONV_HIST - (CONV_W - 1))
                    acc = acc + cw_ref[j:j + 1, cs] * win[off - phase:off - phase + rows]
            conv_sc[r0:r0 + rows, cs] = acc
    acc = conv_sc[...]
    mu = jnp.mean(acc, axis=-1, keepdims=True)
    cen = acc - mu
    var = jnp.mean(cen * cen, axis=-1, keepdims=True)
    c = _silu(cen * lax.rsqrt(var + EPS) * lg_ref[...] + lb_ref[...])
    c = (c * _silu(proj(_O_CZ, D_MODEL))).astype(BF16)
    xq = proj(_O_XQ, MIX_W)
    xqg = xqg_ref[...] * ((HEAD_W ** -0.5) * LOG2E)
    xq = jnp.concatenate([_rms_rows(xq[:, h * HEAD_W:(h + 1) * HEAD_W], xqg) for h in range(N_HEADS)], axis=-1)
    ox = _cross_attn(xq.astype(BF16), _silu(proj(_O_XZ, MIX_W)), mk_ref[0], mv_ref[0])
    y_ref[0] = x + _dot(jnp.concatenate([c, ox], axis=-1), w_out_ref[...])
    tail = u_sc[tm:tm + CONV_HIST, :]
    u_sc[0:CONV_HIST, :] = tail
    tail_ref[0] = tail


def _odd_layer(x, hist, norm_g, w_in, conv_w, conv_b, ln_g, ln_b, xqg, mem_k, mem_v, w_out, tm):
    bsz, seq, _ = x.shape
    tm = min(tm, seq)
    tok = pl.BlockSpec((1, tm, D_MODEL), lambda b, i: (b, i, 0))
    hist_spec = pl.BlockSpec((1, CONV_HIST, D_MODEL), lambda b, i: (b, 0, 0))
    mem_spec = pl.BlockSpec((1, N_MEM, MIX_W), lambda b, i: (b, 0, 0))
    vec = _const_spec((1, D_MODEL))
    return pl.pallas_call(
        functools.partial(_odd_kernel, tm=tm),
        grid=(bsz, seq // tm),
        in_specs=[tok, hist_spec, vec, _const_spec(w_in.shape), _const_spec(conv_w.shape), vec, vec, vec,
                  _const_spec((1, HEAD_W)), mem_spec, mem_spec, _const_spec(w_out.shape)],
        out_specs=[tok, hist_spec],
        out_shape=[jax.ShapeDtypeStruct(x.shape, F32), jax.ShapeDtypeStruct((bsz, CONV_HIST, D_MODEL), F32)],
        scratch_shapes=[pltpu.VMEM((CONV_HIST + tm, D_MODEL), F32), pltpu.VMEM((tm, D_MODEL), F32)],
        compiler_params=_params(("arbitrary", "arbitrary")),
        name="odd_layer",
    )(x, hist, norm_g, w_in, conv_w, conv_b, ln_g, ln_b, xqg, mem_k, mem_v, w_out)


def _even_layer(x, mem_k, mem_v, hist, wts, lam_init, tm_in, tm_out, chunk, tq):
    (norm_g, w_main, w_gate, gate_bias, grp, qn, kn, xqg, mlstm_g, slopes, lam_vecs, subln_g, w_out) = wts
    bsz, seq, _ = x.shape
    (qa, ka, va, ga, gates, qb, kb, kbh, vb, vbh, gb, xq, gx) = _in_even(
        x.reshape(bsz * seq, D_MODEL), norm_g, w_main, w_gate, gate_bias, grp, qn, kn, xqg, tm_in,
        v_transposed_seq=seq if hist is None else None)
    tok = lambda a: a.reshape(bsz, seq, a.shape[-1])
    if hist is None:
        state0 = jnp.zeros((bsz, N_HEADS, HEAD_W, 2 * HEAD_W), F32)
        m0 = jnp.zeros((bsz, N_HEADS, 1, HEAD_W), F32)
    else:
        k_past, v_past, c0, n0, m0 = hist
        state0 = jnp.concatenate([c0, n0[..., None], jnp.zeros(c0.shape[:-1] + (HEAD_W - 1,), F32)], axis=-1)
        m0 = jnp.broadcast_to(m0[..., None, None], (bsz, N_HEADS, 1, HEAD_W))
    oa, state1, m1 = _mlstm(tok(qa), tok(ka), tok(va), tok(gates), tok(ga), mlstm_g, state0, m0, chunk,
                            rows=min(bsz, MLSTM_ROWS))
    if hist is None:
        ob = _diff_prompt(tok(qb), tok(kbh), vbh, tok(gb), lam_vecs, subln_g, lam_init, tq, heads=2)
    else:
        ob = _diff_sample(tok(qb), tok(kbh), tok(vbh), k_past, v_past, tok(gb), slopes, lam_vecs, subln_g,
                          lam_init)
    y = _out_even(x, oa, ob, tok(xq), tok(gx), mem_k, mem_v, w_out, tm_out)
    return (y, kb.reshape(bsz, seq, N_HEADS, HEAD_W), vb.reshape(bsz, seq, N_HEADS, HEAD_W),
            state1[..., :HEAD_W], state1[..., HEAD_W], m1[:, :, 0, 0])


def kernel(x_prompt, x_sample, mem_prompt, cache_xk, cache_xv, cache_k, cache_v, state_C, state_n, state_m,
           state_conv, norm_g, w_in_a, b_ig, b_fg, mlstm_norm_g, qn_g, kn_g, lam_q1, lam_k1, lam_q2, lam_k2,
           subln_g, w_out_a, w_in_c, conv_w, conv_b, conv_ln_g, conv_ln_b, w_out_c, mem_norm_g, w_mem_kv,
           xq_norm_g, xk_norm_g):
    depth = norm_g.shape[0]
    bsz = x_prompt.shape[0]
    dec = x_sample.shape[0]
    p_xk, p_xv = _mem_kv(mem_prompt, mem_norm_g, w_mem_kv, xk_norm_g)
    mem_k_p, mem_v_p = p_xk.astype(BF16), p_xv.astype(BF16)
    mem_k_s = cache_xk.reshape(depth, dec, N_MEM, MIX_W).astype(BF16)
    mem_v_s = cache_xv.reshape(depth, dec, N_MEM, MIX_W).astype(BF16)

    lane_grp = jnp.arange(MIX_W) // DK_B
    grp = (lane_grp[:, None] == lane_grp[None, :]).astype(BF16)
    slopes = jnp.broadcast_to(jnp.array(ALIBI_SLOPES, F32)[:, None, None] * LOG2E, (N_HEADS, 1, HEAD_W))
    n_gate = 2 * N_HEADS
    gate0 = 5 * MIX_W

    yp, ys = x_prompt, x_sample
    outs = {name: [] for name in ("p_k", "p_v", "p_C", "p_n", "p_m", "p_conv",
                                  "s_k", "s_v", "s_C", "s_n", "s_m", "s_conv")}
    for layer in range(depth):
        if layer % 2 == 0:
            e = layer // 2
            w = w_in_a[e]
            w_main = jnp.concatenate([w[:, :gate0], w[:, gate0 + n_gate:]], axis=1).astype(BF16)
            w_gate = jnp.pad(w[:, gate0:gate0 + n_gate], ((0, 0), (0, HEAD_W - n_gate))).astype(BF16)
            gate_bias = jnp.pad(jnp.concatenate([b_ig[e], b_fg[e]]), (0, HEAD_W - n_gate)).reshape(1, HEAD_W)
            wts = (norm_g[layer].reshape(1, D_MODEL), w_main, w_gate, gate_bias, grp,
                   jnp.tile(qn_g[e], MIX_W // DK_B).reshape(1, MIX_W),
                   jnp.tile(kn_g[e], MIX_W // DK_B).reshape(1, MIX_W),
                   xq_norm_g[layer].reshape(1, HEAD_W), mlstm_norm_g[e].reshape(1, MIX_W), slopes,
                   tuple(v[e].reshape(1, DK_B) for v in (lam_q1, lam_k1, lam_q2, lam_k2)),
                   subln_g[e].reshape(1, HEAD_W), w_out_a[e].astype(BF16))
            lam_init = _lambda_init(layer)
            yp, k_new, v_new, c1, n1, m1 = _even_layer(yp, mem_k_p[layer], mem_v_p[layer], None, wts, lam_init,
                                                       tm_in=256, tm_out=512, chunk=256, tq=512)
            for name, val in zip(("p_k", "p_v", "p_C", "p_n", "p_m"), (k_new, v_new, c1, n1, m1)):
                outs[name].append(val)
            hist = (cache_k[e], cache_v[e], state_C[e], state_n[e], state_m[e])
            ys, k_new, v_new, c1, n1, m1 = _even_layer(ys, mem_k_s[layer], mem_v_s[layer], hist, wts, lam_init,
                                                       tm_in=256, tm_out=512, chunk=256, tq=512)
            for name, val in zip(("s_k", "s_v", "s_C", "s_n", "s_m"), (k_new, v_new, c1, n1, m1)):
                outs[name].append(val)
        else:
            o = layer // 2
            vec = lambda a: a.reshape(1, -1)
            wts = (vec(norm_g[layer]), w_in_c[o].astype(BF16), jnp.pad(conv_w[o], ((0, 1), (0, 0))),
                   vec(conv_b[o]), vec(conv_ln_g[o]), vec(conv_ln_b[o]), vec(xq_norm_g[layer]))
            w_out = w_out_c[o].astype(BF16)
            pad = CONV_HIST - (CONV_W - 1)
            zero_hist = jnp.zeros((bsz, CONV_HIST, D_MODEL), F32)
            yp, tail = _odd_layer(yp, zero_hist, *wts, mem_k_p[layer], mem_v_p[layer], w_out, tm=256)
            outs["p_conv"].append(tail[:, pad:])
            hist = jnp.pad(state_conv[o], ((0, 0), (pad, 0), (0, 0)))
            ys, tail = _odd_layer(ys, hist, *wts, mem_k_s[layer], mem_v_s[layer], w_out, tm=256)
            outs["s_conv"].append(tail[:, pad:])

    head5 = lambda a: a.reshape(a.shape[:-1] + (N_HEADS, HEAD_W))
    st = {name: jnp.stack(vals) for name, vals in outs.items()}
    return (yp, ys, head5(p_xk), head5(p_xv), st["p_k"], st["p_v"], st["p_C"], st["p_n"], st["p_m"], st["p_conv"],
            st["s_k"], st["s_v"], st["s_C"], st["s_n"], st["s_m"], st["s_conv"])
```

```python
import functools
import math

import jax
import jax.numpy as jnp
from jax import lax
from jax.experimental import pallas as pl
from jax.experimental.pallas import tpu as pltpu

F32 = jnp.float32
BF16 = jnp.bfloat16

D_MODEL = 1024
CHUNK = 64
EPS = 1e-6
N_HEADS = 4
HEAD_W = 128
MIX_W = N_HEADS * HEAD_W
DK_B = 64
CONV_W = 31
CONV_HIST = 32
CONV_ROWS = 64
SUBLANES = 8
MLSTM_ROWS = 1
N_MEM = 256
ALIBI_SLOPES = tuple(2.0 ** (-8.0 * (h + 1) / N_HEADS) for h in range(N_HEADS))
VMEM_LIMIT = 56 * 1024 * 1024

NEG_INF = float("-inf")
LOG2E = math.log2(math.e)
FIXED_SHIFT_MAX = 32.0
ACC_ROWS = HEAD_W + 16


def _lambda_init(layer):
    return 0.8 - 0.6 * math.exp(-0.3 * layer)


def _params(sem):
    return pltpu.CompilerParams(dimension_semantics=sem, vmem_limit_bytes=VMEM_LIMIT)


def _const_spec(shape):
    zeros = (0,) * len(shape)
    return pl.BlockSpec(shape, lambda *_: zeros)


def _rms_rows(x, g):
    return x * lax.rsqrt(jnp.mean(x * x, axis=-1, keepdims=True) + EPS) * g


def _sigmoid(x):
    return 1.0 / (1.0 + jnp.exp2(x * (-LOG2E)))


def _silu(x):
    return x * _sigmoid(x)


def _dot(a, b):
    return jnp.dot(a, b, preferred_element_type=F32)


def _dot_nt(a, b):
    return lax.dot_general(a, b, (((1,), (1,)), ((), ())), preferred_element_type=F32)


def _dot_tn(a, b):
    return lax.dot_general(a, b, (((0,), (0,)), ((), ())), preferred_element_type=F32)


def _ones_col(rows):
    lane = lax.broadcasted_iota(jnp.int32, (rows, HEAD_W), 1)
    return jnp.where(lane == 0, 1.0, 0.0).astype(BF16)


def _mem_kv_kernel(mem_ref, g_ref, w_ref, kg_ref, k_ref, v_ref):
    xn = _rms_rows(mem_ref[0], g_ref[0]).astype(BF16)
    kv = _dot(xn, w_ref[0])
    kg = kg_ref[0]
    for h in range(N_HEADS):
        sl = slice(h * HEAD_W, (h + 1) * HEAD_W)
        k_ref[0, 0, :, sl] = _rms_rows(kv[:, sl], kg)
    v_ref[0, 0] = kv[:, MIX_W:]


def _mem_kv(mem, g, w_kv, kg):
    depth = w_kv.shape[0]
    bsz, n_mem, _ = mem.shape
    out = jax.ShapeDtypeStruct((depth, bsz, n_mem, MIX_W), F32)
    return pl.pallas_call(
        _mem_kv_kernel,
        grid=(depth, bsz),
        in_specs=[
            pl.BlockSpec((1, n_mem, D_MODEL), lambda l, b: (b, 0, 0)),
            pl.BlockSpec((1, 1, D_MODEL), lambda l, b: (l, 0, 0)),
            pl.BlockSpec((1, D_MODEL, 2 * MIX_W), lambda l, b: (l, 0, 0)),
            pl.BlockSpec((1, 1, HEAD_W), lambda l, b: (l, 0, 0)),
        ],
        out_specs=[pl.BlockSpec((1, 1, n_mem, MIX_W), lambda l, b: (l, b, 0, 0))] * 2,
        out_shape=[out, out],
        compiler_params=_params(("arbitrary", "arbitrary")),
        name="mem_kv",
    )(mem, g.reshape(depth, 1, D_MODEL), w_kv.astype(BF16), kg.reshape(depth, 1, HEAD_W))


_E_AQ, _E_AK, _E_AV, _E_AO, _E_AZ, _E_BQ, _E_BK, _E_BV, _E_BZ, _E_XQ, _E_XZ = (i * MIX_W for i in range(11))
_E_MAIN = 11 * MIX_W


def _group_mean_sq(y, grp):
    sq = y * y
    hi = sq.astype(BF16)
    lo = (sq - hi.astype(F32)).astype(BF16)
    return (_dot(hi, grp) + _dot(lo, grp)) * (1.0 / DK_B)


def _store_head_rows(ref, x):
    tokens = x.shape[0]
    for h in range(N_HEADS):
        ref[pl.ds(h, tokens, stride=N_HEADS), :] = x[:, h * HEAD_W:(h + 1) * HEAD_W]


def _in_even_kernel(x_ref, g_ref, w_ref, wg_ref, gbias_ref, grp_ref, qn_ref, kn_ref, xqg_ref,
                    qa_ref, ka_ref, va_ref, ga_ref, gt_ref, qb_ref, kb_ref, kbh_ref, vb_ref, vbh_ref,
                    gb_ref, xq_ref, gx_ref):
    xn = _rms_rows(x_ref[...], g_ref[...]).astype(BF16)

    def proj(c0):
        return _dot(xn, w_ref[:, c0:c0 + MIX_W])

    qa_ref[...] = proj(_E_AQ).astype(BF16)
    ka_ref[...] = (proj(_E_AK) * (HEAD_W ** -0.5)).astype(BF16)
    va_ref[...] = proj(_E_AV).astype(BF16)
    ga_ref[...] = (_sigmoid(proj(_E_AO)) * _silu(proj(_E_AZ))).astype(BF16)
    gt = _dot(xn, wg_ref[...]) + gbias_ref[...]
    lane = lax.broadcasted_iota(jnp.int32, gt.shape, 1)
    log_sig = jnp.minimum(gt, 0.0) - jnp.log1p(jnp.exp(-jnp.abs(gt)))
    gt_ref[...] = jnp.where((lane >= N_HEADS) & (lane < 2 * N_HEADS), log_sig, gt)
    grp = grp_ref[...]
    bq = proj(_E_BQ)
    q_scale = (DK_B ** -0.5) * LOG2E
    qb_ref[...] = (bq * lax.rsqrt(_group_mean_sq(bq, grp) + EPS) * qn_ref[...] * q_scale).astype(BF16)
    bk = proj(_E_BK)
    kb = bk * lax.rsqrt(_group_mean_sq(bk, grp) + EPS) * kn_ref[...]
    _store_head_rows(kb_ref, kb)
    kbh_ref[...] = kb.astype(BF16)
    bv = proj(_E_BV)
    _store_head_rows(vb_ref, bv)
    if len(vbh_ref.shape) == 3:
        vbh_ref[0] = bv.T.astype(BF16)
    else:
        vbh_ref[...] = bv.astype(BF16)
    gb_ref[...] = _silu(proj(_E_BZ)).astype(BF16)
    xq = proj(_E_XQ)
    xqg = xqg_ref[...] * ((HEAD_W ** -0.5) * LOG2E)
    for h in range(N_HEADS):
        sl = slice(h * HEAD_W, (h + 1) * HEAD_W)
        xq_ref[:, sl] = _rms_rows(xq[:, sl], xqg).astype(BF16)
    gx_ref[...] = _silu(proj(_E_XZ)).astype(BF16)


def _in_even(x2d, norm_g, w_main, w_gate, gate_bias, grp, qn, kn, xqg, tm, v_transposed_seq):
    m = x2d.shape[0]
    tm = min(tm, m)
    row = lambda width: pl.BlockSpec((tm, width), lambda i: (i, 0))
    half = jax.ShapeDtypeStruct((m, MIX_W), BF16)
    full = jax.ShapeDtypeStruct((m * N_HEADS, HEAD_W), F32)
    head_rows = pl.BlockSpec((tm * N_HEADS, HEAD_W), lambda i: (i, 0))
    out_shape = [half, half, half, half, jax.ShapeDtypeStruct((m, HEAD_W), F32),
                 half, full, half, full, half, half, half, half]
    out_specs = [row(MIX_W)] * 4 + [row(HEAD_W)] + [row(MIX_W)] * 8
    out_specs[6] = out_specs[8] = head_rows
    if v_transposed_seq is not None:
        tiles = v_transposed_seq // tm
        out_shape[9] = jax.ShapeDtypeStruct((m // v_transposed_seq, MIX_W, v_transposed_seq), BF16)
        out_specs[9] = pl.BlockSpec((1, MIX_W, tm), lambda i: (i // tiles, 0, i % tiles))
    return pl.pallas_call(
        _in_even_kernel,
        grid=(m // tm,),
        in_specs=[
            row(D_MODEL),
            _const_spec((1, D_MODEL)),
            _const_spec((D_MODEL, _E_MAIN)),
            _const_spec((D_MODEL, HEAD_W)),
            _const_spec((1, HEAD_W)),
            _const_spec((MIX_W, MIX_W)),
            _const_spec((1, MIX_W)),
            _const_spec((1, MIX_W)),
            _const_spec((1, HEAD_W)),
        ],
        out_specs=out_specs,
        out_shape=out_shape,
        compiler_params=_params(("arbitrary",)),
        name="in_even",
    )(x2d, norm_g, w_main, w_gate, gate_bias, grp, qn, kn, xqg)


def _mlstm_kernel(q_ref, k_ref, v_ref, gt_ref, ga_ref, ng_ref, s0_ref, m0_ref, o_ref, s_ref, m_ref, *, chunk):
    c = pl.program_id(1)

    @pl.when(c == 0)
    def _():
        s_ref[...] = s0_ref[...]
        m_ref[...] = m0_ref[...]

    row = lax.broadcasted_iota(jnp.int32, (chunk, chunk), 0)
    col = lax.broadcasted_iota(jnp.int32, (chunk, chunk), 1)
    causal = col <= row
    ones = _ones_col(chunk)
    pairs = [(b, h) for b in range(q_ref.shape[0]) for h in range(N_HEADS)]
    states = {bh: s_ref[bh] for bh in pairs}
    m_prev = {bh: m_ref[bh][:, 0:1] for bh in pairs}
    results = {}
    for b in range(q_ref.shape[0]):
        gt = gt_ref[b]
        cum = jnp.dot(causal.astype(F32), gt, preferred_element_type=F32,
                      precision=lax.Precision.HIGHEST)
        for h in range(N_HEADS):
            results[b, h] = _mlstm_chain(b, h, chunk, causal, ones, gt, cum, gt.T, cum.T, states[b, h],
                                         m_prev[b, h], q_ref, k_ref, v_ref, ga_ref, ng_ref)
    for (b, h), (out, state, m_last) in results.items():
        o_ref[b, :, h * HEAD_W:(h + 1) * HEAD_W] = out
        s_ref[b, h] = state
        m_ref[b, h] = jnp.broadcast_to(m_last, (1, HEAD_W))


def _mlstm_chain(b, h, chunk, causal, ones, gt, cum, gt_t, cum_t, state, m0, q_ref, k_ref, v_ref, ga_ref, ng_ref):
    sl = slice(h * HEAD_W, (h + 1) * HEAD_W)
    b_col = cum[:, N_HEADS + h:N_HEADS + h + 1]
    i_col = gt[:, h:h + 1]
    r_row = gt_t[h:h + 1, :] - cum_t[N_HEADS + h:N_HEADS + h + 1, :]
    g_col = b_col + m0
    logw = jnp.where(causal, b_col + r_row, NEG_INF)
    m_col = jnp.maximum(g_col, jnp.max(logw, axis=-1, keepdims=True))
    w_intra = jnp.exp(logw - m_col)
    w_inter = jnp.exp(g_col - m_col)
    q = q_ref[b, :, sl]
    k = k_ref[b, :, sl]
    v_aug = jnp.concatenate([v_ref[b, :, sl], ones], axis=-1)
    sc = (_dot_nt(q, k) * w_intra).astype(BF16)
    inter = _dot(q, state.astype(BF16))
    intra = _dot(sc, v_aug)
    num = w_inter * inter[:, :HEAD_W] + intra[:, :HEAD_W]
    den = w_inter * inter[:, HEAD_W:HEAD_W + 1] + intra[:, HEAD_W:HEAD_W + 1]
    hid = num / jnp.maximum(jnp.abs(den), jnp.exp(-m_col))
    out = _rms_rows(hid, ng_ref[:, sl]) * ga_ref[b, :, sl].astype(F32)
    m_last = m_col[chunk - 1:chunk, :]
    decay = jnp.exp(g_col[chunk - 1:chunk, :] - m_last)
    w_end = jnp.exp(b_col[chunk - 1:chunk, :] - b_col + i_col - m_last)
    kv = _dot_tn(k, (v_aug.astype(F32) * w_end).astype(BF16))
    return out.astype(BF16), decay * state + kv, m_last


def _mlstm(q, k, v, gates, gate_a, norm_g, state0, m0, chunk, rows):
    bsz, seq, _ = q.shape
    chunk = min(chunk, seq)
    tok = lambda width: pl.BlockSpec((rows, chunk, width), lambda b, c: (b, c, 0))
    st_spec = pl.BlockSpec((rows, N_HEADS, HEAD_W, 2 * HEAD_W), lambda b, c: (b, 0, 0, 0))
    m_spec = pl.BlockSpec((rows, N_HEADS, 1, HEAD_W), lambda b, c: (b, 0, 0, 0))
    return pl.pallas_call(
        functools.partial(_mlstm_kernel, chunk=chunk),
        grid=(bsz // rows, seq // chunk),
        in_specs=[tok(MIX_W), tok(MIX_W), tok(MIX_W), tok(HEAD_W), tok(MIX_W),
                  _const_spec((1, MIX_W)), st_spec, m_spec],
        out_specs=[tok(MIX_W), st_spec, m_spec],
        out_shape=[jax.ShapeDtypeStruct((bsz, seq, MIX_W), BF16),
                   jax.ShapeDtypeStruct(state0.shape, F32),
                   jax.ShapeDtypeStruct(m0.shape, F32)],
        compiler_params=_params(("arbitrary", "arbitrary")),
        name="mlstm",
    )(q, k, v, gates, gate_a, norm_g, state0, m0)


def _split_maps(q):
    lane = lax.broadcasted_iota(jnp.int32, q.shape, 1)
    zero = jnp.zeros_like(q)
    return jnp.concatenate([jnp.where(lane < DK_B, q, zero), jnp.where(lane >= DK_B, q, zero)], axis=0)


def _diff_lambda(lam_refs, lam_init):
    lq1, lk1, lq2, lk2 = (r[...] for r in lam_refs)
    return (jnp.exp(jnp.sum(lq1 * lk1, axis=-1, keepdims=True))
            - jnp.exp(jnp.sum(lq2 * lk2, axis=-1, keepdims=True)) + lam_init)


def _diff_gate(o, lam_init, sg, gate):
    return (_rms_rows(o, sg) * (1.0 - lam_init) * gate.astype(F32)).astype(BF16)


def _diff_exact_kernel(q_ref, k_ref, vt_ref, gate_ref, bias_ref, lq1_ref, lk1_ref, lq2_ref, lk2_ref,
                       sg_ref, o_ref, m_sc, acc_sc, *, tq, heads, lam_init):
    i = pl.program_id(2)
    tiles_per_map = tq // HEAD_W
    m_sc[...] = jnp.full(m_sc.shape, NEG_INF, F32)
    acc_sc[...] = jnp.zeros(acc_sc.shape, F32)
    ones_rows = jnp.ones((ACC_ROWS - HEAD_W, tq), BF16)
    qq = [_split_maps(q_ref[0, :, h * HEAD_W:(h + 1) * HEAD_W]) for h in range(heads)]

    def block(j, table):
        start = pl.multiple_of(j * tq, tq)
        scores = [_dot_nt(k_ref[0, pl.ds(start, tq), h * HEAD_W:(h + 1) * HEAD_W], qq[h]) for h in range(heads)]
        for h in range(heads):
            hs = slice(h * HEAD_W, (h + 1) * HEAD_W)
            offset = bias_ref[h, 0, 1:2, 0:1] * ((j - i) * tq).astype(F32)
            v_aug = jnp.concatenate([vt_ref[0, hs, pl.ds(start, tq)], ones_rows], axis=0)
            m_old = m_sc[h]
            m_new, p = [], []
            for t in range(2 * tiles_per_map):
                cs = slice(t * HEAD_W, (t + 1) * HEAD_W)
                part = (t % tiles_per_map) * HEAD_W
                s = scores[h][:, cs] + bias_ref[h, table, :, part:part + HEAD_W]
                m_t = jnp.maximum(m_old[:, cs], jnp.max(s, axis=0, keepdims=True) + offset)
                p.append(jnp.exp2(s - (m_t - offset)).astype(BF16))
                m_new.append(m_t)
            m_new = jnp.concatenate(m_new, axis=-1)
            acc_sc[h] = jnp.exp2(m_old - m_new) * acc_sc[h] + _dot(v_aug, jnp.concatenate(p, axis=-1))
            m_sc[h] = m_new

    def earlier(j, carry):
        block(j, 0)
        return carry

    lax.fori_loop(0, i, earlier, 0)
    block(i, 1)
    _diff_finish(acc_sc, tq, heads, (lq1_ref, lk1_ref, lq2_ref, lk2_ref), lam_init, sg_ref, gate_ref, o_ref)


def _diff_fixed_kernel(q_ref, qpos_ref, k_ref, kpos_ref, vt_ref, gate_ref, corr_ref, lq1_ref, lk1_ref, lq2_ref,
                       lk2_ref, sg_ref, o_ref, acc_sc, *, tq, heads, lam_init):
    i = pl.program_id(2)
    tiles_per_map = tq // HEAD_W
    acc_sc[...] = jnp.zeros(acc_sc.shape, F32)
    ones_rows = jnp.ones((ACC_ROWS - HEAD_W, tq), BF16)
    qq = [jnp.concatenate([_split_maps(q_ref[0, :, h * HEAD_W:(h + 1) * HEAD_W]),
                           jnp.concatenate([qpos_ref[h], qpos_ref[h]], axis=0)], axis=-1)
          for h in range(heads)]

    def block(j, diagonal):
        start = pl.multiple_of(j * tq, tq)
        for h in range(heads):
            hs = slice(h * HEAD_W, (h + 1) * HEAD_W)
            k_aug = jnp.concatenate([k_ref[0, pl.ds(start, tq), hs], kpos_ref[h, pl.ds(start, tq), :]], axis=-1)
            s = _dot_nt(k_aug, qq[h])
            if diagonal:
                s = jnp.concatenate(
                    [s[:, t * HEAD_W:(t + 1) * HEAD_W]
                     + corr_ref[h, :, (t % tiles_per_map) * HEAD_W:(t % tiles_per_map + 1) * HEAD_W]
                     for t in range(2 * tiles_per_map)], axis=-1)
            v_aug = jnp.concatenate([vt_ref[0, hs, pl.ds(start, tq)], ones_rows], axis=0)
            acc_sc[h] = acc_sc[h] + _dot(v_aug, jnp.exp2(s).astype(BF16))

    def earlier(j, carry):
        block(j, False)
        return carry

    lax.fori_loop(0, i, earlier, 0)
    block(i, True)
    _diff_finish(acc_sc, tq, heads, (lq1_ref, lk1_ref, lq2_ref, lk2_ref), lam_init, sg_ref, gate_ref, o_ref)


def _diff_finish(acc_sc, tq, heads, lam_refs, lam_init, sg_ref, gate_ref, o_ref):
    lam = _diff_lambda(lam_refs, lam_init)
    for h in range(heads):
        hs = slice(h * HEAD_W, (h + 1) * HEAD_W)
        acc = acc_sc[h]
        o_t = (acc[:HEAD_W, :tq] / acc[HEAD_W:HEAD_W + 1, :tq]
               - lam * (acc[:HEAD_W, tq:] / acc[HEAD_W:HEAD_W + 1, tq:]))
        o_ref[0, :, hs] = _diff_gate(o_t.T, lam_init, sg_ref[...], gate_ref[0, :, hs])


def _alibi_tables(tq):
    slopes = jnp.array(ALIBI_SLOPES, F32) * LOG2E
    kk = jnp.arange(tq, dtype=jnp.int32)[:, None]
    qi = jnp.arange(tq, dtype=jnp.int32)[None, :]
    base = slopes[:, None, None] * jnp.broadcast_to(kk.astype(F32), (tq, tq))[None]
    diag = slopes[:, None, None] * (qi - jnp.abs(qi - kk)).astype(F32)[None]
    diag = jnp.where((kk // CHUNK <= qi // CHUNK)[None], diag, NEG_INF)
    return jnp.stack([base, diag], axis=1)


def _truncate_to_bf16_grid(x):
    bits = lax.bitcast_convert_type(x, jnp.uint32) & jnp.uint32(0xFFFF0000)
    return lax.bitcast_convert_type(bits, F32)


def _split3(x):
    hi = _truncate_to_bf16_grid(x)
    mid = _truncate_to_bf16_grid(x - hi)
    return hi.astype(BF16), mid.astype(BF16), (x - hi - mid).astype(BF16)


def _position_tables(seq, tq, ub):
    slopes = jnp.array(ALIBI_SLOPES, F32)[:, None] * LOG2E
    pos = jnp.arange(seq, dtype=F32)[None, :]
    ones = jnp.ones((N_HEADS, seq), BF16)
    lanes = lambda cols: jnp.pad(jnp.stack(cols, axis=-1), ((0, 0), (0, 0), (0, HEAD_W - len(cols))))
    kpos = lanes([*_split3(slopes * pos), ones, ones, ones])
    qpos = lanes([ones, ones, ones, *_split3(-slopes * pos - ub)])
    kk = jnp.arange(tq, dtype=jnp.int32)[:, None]
    qi = jnp.arange(tq, dtype=jnp.int32)[None, :]
    corr = slopes[:, :, None] * (-2.0 * jnp.maximum(kk - qi, 0).astype(F32))[None]
    corr = jnp.where((kk // CHUNK <= qi // CHUNK)[None], corr, NEG_INF)
    return kpos, qpos, corr


def _diff_prompt(q, k, vt, gate, lam_vecs, subln_g, lam_init, tq, heads, ub):
    bsz, seq, _ = q.shape
    tq = min(tq, seq)
    width = heads * HEAD_W
    grid = (bsz, N_HEADS // heads, seq // tq)
    q_spec = pl.BlockSpec((1, tq, width), lambda b, h, i: (b, i, h))
    once = dict(pipeline_mode=pl.Buffered(1))
    k_spec = pl.BlockSpec((1, seq, width), lambda b, h, i: (b, 0, h), **once)
    vt_spec = pl.BlockSpec((1, width, seq), lambda b, h, i: (b, h, 0), **once)
    tail_specs = [_const_spec((1, DK_B))] * 4 + [_const_spec((1, HEAD_W))]
    out_shape = jax.ShapeDtypeStruct((bsz, seq, MIX_W), BF16)
    acc_buf = pltpu.VMEM((heads, ACC_ROWS, 2 * tq), F32)
    sem = _params(("arbitrary", "arbitrary", "arbitrary"))

    def fixed():
        kpos, qpos, corr = _position_tables(seq, tq, ub)
        return pl.pallas_call(
            functools.partial(_diff_fixed_kernel, tq=tq, heads=heads, lam_init=lam_init),
            grid=grid,
            in_specs=[q_spec,
                      pl.BlockSpec((heads, tq, HEAD_W), lambda b, h, i: (h, i, 0)),
                      k_spec,
                      pl.BlockSpec((heads, seq, HEAD_W), lambda b, h, i: (h, 0, 0), **once),
                      vt_spec, q_spec,
                      pl.BlockSpec((heads, tq, tq), lambda b, h, i: (h, 0, 0), **once)] + tail_specs,
            out_specs=q_spec, out_shape=out_shape, scratch_shapes=[acc_buf],
            compiler_params=sem, name="diff_fixed",
        )(q, qpos, k, kpos, vt, gate, corr, *lam_vecs, subln_g)

    def exact():
        return pl.pallas_call(
            functools.partial(_diff_exact_kernel, tq=tq, heads=heads, lam_init=lam_init),
            grid=grid,
            in_specs=[q_spec, k_spec, vt_spec, q_spec,
                      pl.BlockSpec((heads, 2, tq, tq), lambda b, h, i: (h, 0, 0, 0), **once)] + tail_specs,
            out_specs=q_spec, out_shape=out_shape,
            scratch_shapes=[pltpu.VMEM((heads, 1, 2 * tq), F32), acc_buf],
            compiler_params=sem, name="diff_exact",
        )(q, k, vt, gate, _alibi_tables(tq), *lam_vecs, subln_g)

    return lax.cond(ub <= FIXED_SHIFT_MAX, fixed, exact)


def _diff_sample_kernel(q_ref, k_ref, v_ref, kp_ref, vp_ref, gate_ref, slope_ref, lq1_ref, lk1_ref, lq2_ref,
                        lk2_ref, sg_ref, o_ref, *, lam_init):
    rows = q_ref.shape[1]
    past = kp_ref.shape[1] // N_HEADS
    lam = _diff_lambda((lq1_ref, lk1_ref, lq2_ref, lk2_ref), lam_init)
    r = lax.broadcasted_iota(jnp.int32, (2 * rows, 1), 0)
    r = jnp.where(r >= rows, r - rows, r)
    dist_past = (past + r - lax.broadcasted_iota(jnp.int32, (2 * rows, past), 1)).astype(F32)
    dist_new = jnp.abs(r - lax.broadcasted_iota(jnp.int32, (2 * rows, rows), 1)).astype(F32)
    for h in range(N_HEADS):
        hs = slice(h * HEAD_W, (h + 1) * HEAD_W)
        slope = slope_ref[h][:, 0:1]
        qq = _split_maps(q_ref[0, :, hs])
        k_past = kp_ref[0, pl.ds(h, past, stride=N_HEADS), :].astype(BF16)
        v_past = vp_ref[0, pl.ds(h, past, stride=N_HEADS), :].astype(BF16)
        s_past = _dot_nt(qq, k_past) - slope * dist_past
        s_new = _dot_nt(qq, k_ref[0, :, hs]) - slope * dist_new
        m = jnp.maximum(jnp.max(s_past, axis=-1, keepdims=True), jnp.max(s_new, axis=-1, keepdims=True))
        acc = (_dot(jnp.exp2(s_past - m).astype(BF16), jnp.concatenate([v_past, _ones_col(past)], axis=-1))
               + _dot(jnp.exp2(s_new - m).astype(BF16), jnp.concatenate([v_ref[0, :, hs], _ones_col(rows)], axis=-1)))
        o = (acc[:rows, :HEAD_W] / acc[:rows, HEAD_W:HEAD_W + 1]
             - lam * (acc[rows:, :HEAD_W] / acc[rows:, HEAD_W:HEAD_W + 1]))
        o_ref[0, :, hs] = _diff_gate(o, lam_init, sg_ref[...], gate_ref[0, :, hs])


def _diff_sample(q, k_new, v_new, k_past, v_past, gate, slopes, lam_vecs, subln_g, lam_init):
    bsz, rows, _ = q.shape
    past = k_past.shape[1]
    tok = pl.BlockSpec((1, rows, MIX_W), lambda b: (b, 0, 0))
    cache = pl.BlockSpec((1, past * N_HEADS, HEAD_W), lambda b: (b, 0, 0))
    head_rows = lambda a: a.reshape(bsz, past * N_HEADS, HEAD_W)
    return pl.pallas_call(
        functools.partial(_diff_sample_kernel, lam_init=lam_init),
        grid=(bsz,),
        in_specs=[tok, tok, tok, cache, cache, tok, _const_spec((N_HEADS, 1, HEAD_W))]
                 + [_const_spec((1, DK_B))] * 4 + [_const_spec((1, HEAD_W))],
        out_specs=tok,
        out_shape=jax.ShapeDtypeStruct((bsz, rows, MIX_W), BF16),
        compiler_params=_params(("arbitrary",)),
        name="diff_sample",
    )(q, k_new, v_new, head_rows(k_past), head_rows(v_past), gate, slopes, *lam_vecs, subln_g)


def _cross_attn(xq, gate, mem_k, mem_v):
    outs = []
    for h in range(N_HEADS):
        sl = slice(h * HEAD_W, (h + 1) * HEAD_W)
        s = _dot_nt(xq[:, sl], mem_k[:, sl])
        p = jnp.exp2(s - jnp.max(s, axis=-1, keepdims=True))
        outs.append(_dot(p.astype(BF16), mem_v[:, sl]) / jnp.sum(p, axis=-1, keepdims=True))
    return (jnp.concatenate(outs, axis=-1) * gate).astype(BF16)


def _out_even_kernel(x_ref, oa_ref, ob_ref, xq_ref, gx_ref, mk_ref, mv_ref, w_ref, y_ref):
    ox = _cross_attn(xq_ref[0], gx_ref[0].astype(F32), mk_ref[0], mv_ref[0])
    mixed = jnp.concatenate([oa_ref[0], ob_ref[0], ox], axis=-1)
    y_ref[0] = x_ref[0] + _dot(mixed, w_ref[...])


def _out_even(x, oa, ob, xq, gx, mem_k, mem_v, w_out, tm):
    bsz, seq, _ = x.shape
    tm = min(tm, seq)
    tok = lambda width: pl.BlockSpec((1, tm, width), lambda b, i: (b, i, 0))
    mem_spec = pl.BlockSpec((1, N_MEM, MIX_W), lambda b, i: (b, 0, 0))
    return pl.pallas_call(
        _out_even_kernel,
        grid=(bsz, seq // tm),
        in_specs=[tok(D_MODEL), tok(MIX_W), tok(MIX_W), tok(MIX_W), tok(MIX_W), mem_spec, mem_spec,
                  _const_spec((3 * MIX_W, D_MODEL))],
        out_specs=tok(D_MODEL),
        out_shape=jax.ShapeDtypeStruct(x.shape, F32),
        compiler_params=_params(("arbitrary", "arbitrary")),
        name="out_even",
    )(x, oa, ob, xq, gx, mem_k, mem_v, w_out)


_O_CU, _O_CG, _O_CZ = 0, D_MODEL, 2 * D_MODEL
_O_XQ, _O_XZ = 3 * D_MODEL, 3 * D_MODEL + MIX_W


def _odd_kernel(x_ref, hist_ref, g_ref, w_in_ref, cw_ref, cb_ref, lg_ref, lb_ref, xqg_ref, mk_ref, mv_ref,
                w_out_ref, y_ref, tail_ref, u_sc, conv_sc, *, tm):
    i = pl.program_id(1)

    @pl.when(i == 0)
    def _():
        u_sc[0:CONV_HIST, :] = hist_ref[0]

    x = x_ref[0]
    xn = _rms_rows(x, g_ref[...]).astype(BF16)

    def proj(c0, width):
        return _dot(xn, w_in_ref[:, c0:c0 + width])

    u = proj(_O_CU, D_MODEL) * _sigmoid(proj(_O_CG, D_MODEL))
    u_sc[CONV_HIST:CONV_HIST + tm, :] = u
    rows = min(CONV_ROWS, tm)
    for c0 in range(0, D_MODEL, HEAD_W):
        cs = slice(c0, c0 + HEAD_W)
        for r0 in range(0, tm, rows):
            acc = jnp.broadcast_to(cb_ref[:, cs], (rows, HEAD_W))
            span = rows + CONV_HIST
            aligned = u_sc[r0:r0 + span, cs]
            for phase in range(SUBLANES):
                offs = [o for o in range(CONV_HIST - (CONV_W - 1), CONV_HIST + 1) if o % SUBLANES == phase]
                win = pltpu.roll(aligned, span - phase, axis=0) if phase else aligned
                for off in offs:
                    j = off - (CONV_HIST - (CONV_W - 1))
                    acc = acc + cw_ref[j:j + 1, cs] * win[off - phase:off - phase + rows]
            conv_sc[r0:r0 + rows, cs] = acc
    acc = conv_sc[...]
    mu = jnp.mean(acc, axis=-1, keepdims=True)
    cen = acc - mu
    var = jnp.mean(cen * cen, axis=-1, keepdims=True)
    c = _silu(cen * lax.rsqrt(var + EPS) * lg_ref[...] + lb_ref[...])
    c = (c * _silu(proj(_O_CZ, D_MODEL))).astype(BF16)
    xq = proj(_O_XQ, MIX_W)
    xqg = xqg_ref[...] * ((HEAD_W ** -0.5) * LOG2E)
    xq = jnp.concatenate([_rms_rows(xq[:, h * HEAD_W:(h + 1) * HEAD_W], xqg) for h in range(N_HEADS)], axis=-1)
    ox = _cross_attn(xq.astype(BF16), _silu(proj(_O_XZ, MIX_W)), mk_ref[0], mv_ref[0])
    y_ref[0] = x + _dot(jnp.concatenate([c, ox], axis=-1), w_out_ref[...])
    tail = u_sc[tm:tm + CONV_HIST, :]
    u_sc[0:CONV_HIST, :] = tail
    tail_ref[0] = tail


def _odd_layer(x, hist, norm_g, w_in, conv_w, conv_b, ln_g, ln_b, xqg, mem_k, mem_v, w_out, tm):
    bsz, seq, _ = x.shape
    tm = min(tm, seq)
    tok = pl.BlockSpec((1, tm, D_MODEL), lambda b, i: (b, i, 0))
    hist_spec = pl.BlockSpec((1, CONV_HIST, D_MODEL), lambda b, i: (b, 0, 0))
    mem_spec = pl.BlockSpec((1, N_MEM, MIX_W), lambda b, i: (b, 0, 0))
    vec = _const_spec((1, D_MODEL))
    return pl.pallas_call(
        functools.partial(_odd_kernel, tm=tm),
        grid=(bsz, seq // tm),
        in_specs=[tok, hist_spec, vec, _const_spec(w_in.shape), _const_spec(conv_w.shape), vec, vec, vec,
                  _const_spec((1, HEAD_W)), mem_spec, mem_spec, _const_spec(w_out.shape)],
        out_specs=[tok, hist_spec],
        out_shape=[jax.ShapeDtypeStruct(x.shape, F32), jax.ShapeDtypeStruct((bsz, CONV_HIST, D_MODEL), F32)],
        scratch_shapes=[pltpu.VMEM((CONV_HIST + tm, D_MODEL), F32), pltpu.VMEM((tm, D_MODEL), F32)],
        compiler_params=_params(("arbitrary", "arbitrary")),
        name="odd_layer",
    )(x, hist, norm_g, w_in, conv_w, conv_b, ln_g, ln_b, xqg, mem_k, mem_v, w_out)


def _even_layer(x, mem_k, mem_v, hist, wts, lam_init, tm_in, tm_out, chunk, tq):
    (norm_g, w_main, w_gate, gate_bias, grp, qn, kn, xqg, mlstm_g, slopes, lam_vecs, subln_g, w_out) = wts
    bsz, seq, _ = x.shape
    (qa, ka, va, ga, gates, qb, kb, kbh, vb, vbh, gb, xq, gx) = _in_even(
        x.reshape(bsz * seq, D_MODEL), norm_g, w_main, w_gate, gate_bias, grp, qn, kn, xqg, tm_in,
        v_transposed_seq=seq if hist is None else None)
    tok = lambda a: a.reshape(bsz, seq, a.shape[-1])
    if hist is None:
        state0 = jnp.zeros((bsz, N_HEADS, HEAD_W, 2 * HEAD_W), F32)
        m0 = jnp.zeros((bsz, N_HEADS, 1, HEAD_W), F32)
    else:
        k_past, v_past, c0, n0, m0 = hist
        state0 = jnp.concatenate([c0, n0[..., None], jnp.zeros(c0.shape[:-1] + (HEAD_W - 1,), F32)], axis=-1)
        m0 = jnp.broadcast_to(m0[..., None, None], (bsz, N_HEADS, 1, HEAD_W))
    oa, state1, m1 = _mlstm(tok(qa), tok(ka), tok(va), tok(gates), tok(ga), mlstm_g, state0, m0, chunk,
                            rows=min(bsz, MLSTM_ROWS))
    if hist is None:
        ub = (8.0 * LOG2E * 1.01) * jnp.max(jnp.abs(qn)) * jnp.max(jnp.abs(kn))
        ob = _diff_prompt(tok(qb), tok(kbh), vbh, tok(gb), lam_vecs, subln_g, lam_init, tq, heads=4, ub=ub)
    else:
        ob = _diff_sample(tok(qb), tok(kbh), tok(vbh), k_past, v_past, tok(gb), slopes, lam_vecs, subln_g,
                          lam_init)
    y = _out_even(x, oa, ob, tok(xq), tok(gx), mem_k, mem_v, w_out, tm_out)
    return (y, kb.reshape(bsz, seq, N_HEADS, HEAD_W), vb.reshape(bsz, seq, N_HEADS, HEAD_W),
            state1[..., :HEAD_W], state1[..., HEAD_W], m1[:, :, 0, 0])


def kernel(x_prompt, x_sample, mem_prompt, cache_xk, cache_xv, cache_k, cache_v, state_C, state_n, state_m,
           state_conv, norm_g, w_in_a, b_ig, b_fg, mlstm_norm_g, qn_g, kn_g, lam_q1, lam_k1, lam_q2, lam_k2,
           subln_g, w_out_a, w_in_c, conv_w, conv_b, conv_ln_g, conv_ln_b, w_out_c, mem_norm_g, w_mem_kv,
           xq_norm_g, xk_norm_g):
    depth = norm_g.shape[0]
    bsz = x_prompt.shape[0]
    dec = x_sample.shape[0]
    p_xk, p_xv = _mem_kv(mem_prompt, mem_norm_g, w_mem_kv, xk_norm_g)
    mem_k_p, mem_v_p = p_xk.astype(BF16), p_xv.astype(BF16)
    mem_k_s = cache_xk.reshape(depth, dec, N_MEM, MIX_W).astype(BF16)
    mem_v_s = cache_xv.reshape(depth, dec, N_MEM, MIX_W).astype(BF16)

    lane_grp = jnp.arange(MIX_W) // DK_B
    grp = (lane_grp[:, None] == lane_grp[None, :]).astype(BF16)
    slopes = jnp.broadcast_to(jnp.array(ALIBI_SLOPES, F32)[:, None, None] * LOG2E, (N_HEADS, 1, HEAD_W))
    n_gate = 2 * N_HEADS
    gate0 = 5 * MIX_W

    yp, ys = x_prompt, x_sample
    outs = {name: [] for name in ("p_k", "p_v", "p_C", "p_n", "p_m", "p_conv",
                                  "s_k", "s_v", "s_C", "s_n", "s_m", "s_conv")}
    for layer in range(depth):
        if layer % 2 == 0:
            e = layer // 2
            w = w_in_a[e]
            w_main = jnp.concatenate([w[:, :gate0], w[:, gate0 + n_gate:]], axis=1).astype(BF16)
            w_gate = jnp.pad(w[:, gate0:gate0 + n_gate], ((0, 0), (0, HEAD_W - n_gate))).astype(BF16)
            gate_bias = jnp.pad(jnp.concatenate([b_ig[e], b_fg[e]]), (0, HEAD_W - n_gate)).reshape(1, HEAD_W)
            wts = (norm_g[layer].reshape(1, D_MODEL), w_main, w_gate, gate_bias, grp,
                   jnp.tile(qn_g[e], MIX_W // DK_B).reshape(1, MIX_W),
                   jnp.tile(kn_g[e], MIX_W // DK_B).reshape(1, MIX_W),
                   xq_norm_g[layer].reshape(1, HEAD_W), mlstm_norm_g[e].reshape(1, MIX_W), slopes,
                   tuple(v[e].reshape(1, DK_B) for v in (lam_q1, lam_k1, lam_q2, lam_k2)),
                   subln_g[e].reshape(1, HEAD_W), w_out_a[e].astype(BF16))
            lam_init = _lambda_init(layer)
            yp, k_new, v_new, c1, n1, m1 = _even_layer(yp, mem_k_p[layer], mem_v_p[layer], None, wts, lam_init,
                                                       tm_in=256, tm_out=512, chunk=256, tq=512)
            for name, val in zip(("p_k", "p_v", "p_C", "p_n", "p_m"), (k_new, v_new, c1, n1, m1)):
                outs[name].append(val)
            hist = (cache_k[e], cache_v[e], state_C[e], state_n[e], state_m[e])
            ys, k_new, v_new, c1, n1, m1 = _even_layer(ys, mem_k_s[layer], mem_v_s[layer], hist, wts, lam_init,
                                                       tm_in=256, tm_out=512, chunk=256, tq=512)
            for name, val in zip(("s_k", "s_v", "s_C", "s_n", "s_m"), (k_new, v_new, c1, n1, m1)):
                outs[name].append(val)
        else:
            o = layer // 2
            vec = lambda a: a.reshape(1, -1)
            wts = (vec(norm_g[layer]), w_in_c[o].astype(BF16), jnp.pad(conv_w[o], ((0, 1), (0, 0))),
                   vec(conv_b[o]), vec(conv_ln_g[o]), vec(conv_ln_b[o]), vec(xq_norm_g[layer]))
            w_out = w_out_c[o].astype(BF16)
            pad = CONV_HIST - (CONV_W - 1)
            zero_hist = jnp.zeros((bsz, CONV_HIST, D_MODEL), F32)
            yp, tail = _odd_layer(yp, zero_hist, *wts, mem_k_p[layer], mem_v_p[layer], w_out, tm=256)
            outs["p_conv"].append(tail[:, pad:])
            hist = jnp.pad(state_conv[o], ((0, 0), (pad, 0), (0, 0)))
            ys, tail = _odd_layer(ys, hist, *wts, mem_k_s[layer], mem_v_s[layer], w_out, tm=256)
            outs["s_conv"].append(tail[:, pad:])

    head5 = lambda a: a.reshape(a.shape[:-1] + (N_HEADS, HEAD_W))
    st = {name: jnp.stack(vals) for name, vals in outs.items()}
    return (yp, ys, head5(p_xk), head5(p_xv), st["p_k"], st["p_v"], st["p_C"], st["p_n"], st["p_m"], st["p_conv"],
            st["s_k"], st["s_v"], st["s_C"], st["s_n"], st["s_m"], st["s_conv"])
```

```python
import functools
import math

import jax
import jax.numpy as jnp
from jax import lax
from jax.experimental import pallas as pl
from jax.experimental.pallas import tpu as pltpu

F32 = jnp.float32
BF16 = jnp.bfloat16

D_MODEL = 1024
CHUNK = 64
EPS = 1e-6
N_HEADS = 4
HEAD_W = 128
MIX_W = N_HEADS * HEAD_W
DK_B = 64
CONV_W = 31
CONV_HIST = 32
CONV_ROWS = 64
SUBLANES = 8
N_MEM = 256
ALIBI_SLOPES = tuple(2.0 ** (-8.0 * (h + 1) / N_HEADS) for h in range(N_HEADS))
VMEM_LIMIT = 56 * 1024 * 1024

NEG_INF = float("-inf")
LOG2E = math.log2(math.e)
FIXED_SHIFT_MAX = 32.0
ACC_ROWS = HEAD_W + 16


def _lambda_init(layer):
    return 0.8 - 0.6 * math.exp(-0.3 * layer)


def _params(sem):
    return pltpu.CompilerParams(dimension_semantics=sem, vmem_limit_bytes=VMEM_LIMIT)


def _const_spec(shape):
    zeros = (0,) * len(shape)
    return pl.BlockSpec(shape, lambda *_: zeros)


def _rms_rows(x, g):
    return x * lax.rsqrt(jnp.mean(x * x, axis=-1, keepdims=True) + EPS) * g


def _sigmoid(x):
    return 1.0 / (1.0 + jnp.exp2(x * (-LOG2E)))


def _silu(x):
    return x * _sigmoid(x)


def _dot(a, b):
    return jnp.dot(a, b, preferred_element_type=F32)


def _dot_nt(a, b):
    return lax.dot_general(a, b, (((1,), (1,)), ((), ())), preferred_element_type=F32)


def _dot_tn(a, b):
    return lax.dot_general(a, b, (((0,), (0,)), ((), ())), preferred_element_type=F32)


def _ones_col(rows):
    lane = lax.broadcasted_iota(jnp.int32, (rows, HEAD_W), 1)
    return jnp.where(lane == 0, 1.0, 0.0).astype(BF16)


def _mem_kv_kernel(mem_ref, g_ref, w_ref, kg_ref, k_ref, v_ref):
    xn = _rms_rows(mem_ref[0], g_ref[0]).astype(BF16)
    kv = _dot(xn, w_ref[0])
    kg = kg_ref[0]
    for h in range(N_HEADS):
        sl = slice(h * HEAD_W, (h + 1) * HEAD_W)
        k_ref[0, 0, :, sl] = _rms_rows(kv[:, sl], kg)
    v_ref[0, 0] = kv[:, MIX_W:]


def _mem_kv(mem, g, w_kv, kg):
    depth = w_kv.shape[0]
    bsz, n_mem, _ = mem.shape
    out = jax.ShapeDtypeStruct((depth, bsz, n_mem, MIX_W), F32)
    return pl.pallas_call(
        _mem_kv_kernel,
        grid=(depth, bsz),
        in_specs=[
            pl.BlockSpec((1, n_mem, D_MODEL), lambda l, b: (b, 0, 0)),
            pl.BlockSpec((1, 1, D_MODEL), lambda l, b: (l, 0, 0)),
            pl.BlockSpec((1, D_MODEL, 2 * MIX_W), lambda l, b: (l, 0, 0)),
            pl.BlockSpec((1, 1, HEAD_W), lambda l, b: (l, 0, 0)),
        ],
        out_specs=[pl.BlockSpec((1, 1, n_mem, MIX_W), lambda l, b: (l, b, 0, 0))] * 2,
        out_shape=[out, out],
        compiler_params=_params(("arbitrary", "arbitrary")),
        name="mem_kv",
    )(mem, g.reshape(depth, 1, D_MODEL), w_kv.astype(BF16), kg.reshape(depth, 1, HEAD_W))


_E_AQ, _E_AK, _E_AV, _E_AO, _E_AZ, _E_BQ, _E_BK, _E_BV, _E_BZ, _E_XQ, _E_XZ = (i * MIX_W for i in range(11))
_E_MAIN = 11 * MIX_W


def _group_mean_sq(y, grp):
    sq = y * y
    hi = sq.astype(BF16)
    lo = (sq - hi.astype(F32)).astype(BF16)
    return (_dot(hi, grp) + _dot(lo, grp)) * (1.0 / DK_B)


def _store_head_rows(ref, x):
    tokens = x.shape[0]
    for h in range(N_HEADS):
        ref[pl.ds(h, tokens, stride=N_HEADS), :] = x[:, h * HEAD_W:(h + 1) * HEAD_W]


def _in_even_kernel(x_ref, g_ref, w_ref, wg_ref, gbias_ref, grp_ref, qn_ref, kn_ref, xqg_ref,
                    qa_ref, ka_ref, va_ref, ga_ref, gt_ref, qb_ref, kb_ref, kbh_ref, vb_ref, vbh_ref,
                    gb_ref, xq_ref, gx_ref):
    xn = _rms_rows(x_ref[...], g_ref[...]).astype(BF16)

    def proj(c0):
        return _dot(xn, w_ref[:, c0:c0 + MIX_W])

    qa_ref[...] = proj(_E_AQ).astype(BF16)
    ka_ref[...] = (proj(_E_AK) * (HEAD_W ** -0.5)).astype(BF16)
    va_ref[...] = proj(_E_AV).astype(BF16)
    ga_ref[...] = (_sigmoid(proj(_E_AO)) * _silu(proj(_E_AZ))).astype(BF16)
    gt = _dot(xn, wg_ref[...]) + gbias_ref[...]
    lane = lax.broadcasted_iota(jnp.int32, gt.shape, 1)
    log_sig = jnp.minimum(gt, 0.0) - jnp.log1p(jnp.exp(-jnp.abs(gt)))
    gt_ref[...] = jnp.where((lane >= N_HEADS) & (lane < 2 * N_HEADS), log_sig, gt)
    grp = grp_ref[...]
    bq = proj(_E_BQ)
    q_scale = (DK_B ** -0.5) * LOG2E
    qb_ref[...] = (bq * lax.rsqrt(_group_mean_sq(bq, grp) + EPS) * qn_ref[...] * q_scale).astype(BF16)
    bk = proj(_E_BK)
    kb = bk * lax.rsqrt(_group_mean_sq(bk, grp) + EPS) * kn_ref[...]
    _store_head_rows(kb_ref, kb)
    kbh_ref[...] = kb.astype(BF16)
    bv = proj(_E_BV)
    _store_head_rows(vb_ref, bv)
    if len(vbh_ref.shape) == 3:
        vbh_ref[0] = bv.T.astype(BF16)
    else:
        vbh_ref[...] = bv.astype(BF16)
    gb_ref[...] = _silu(proj(_E_BZ)).astype(BF16)
    xq = proj(_E_XQ)
    xqg = xqg_ref[...] * ((HEAD_W ** -0.5) * LOG2E)
    for h in range(N_HEADS):
        sl = slice(h * HEAD_W, (h + 1) * HEAD_W)
        xq_ref[:, sl] = _rms_rows(xq[:, sl], xqg).astype(BF16)
    gx_ref[...] = _silu(proj(_E_XZ)).astype(BF16)


def _in_even(x2d, norm_g, w_main, w_gate, gate_bias, grp, qn, kn, xqg, tm, v_transposed_seq):
    m = x2d.shape[0]
    tm = min(tm, m)
    row = lambda width: pl.BlockSpec((tm, width), lambda i: (i, 0))
    half = jax.ShapeDtypeStruct((m, MIX_W), BF16)
    full = jax.ShapeDtypeStruct((m * N_HEADS, HEAD_W), F32)
    head_rows = pl.BlockSpec((tm * N_HEADS, HEAD_W), lambda i: (i, 0))
    out_shape = [half, half, half, half, jax.ShapeDtypeStruct((m, HEAD_W), F32),
                 half, full, half, full, half, half, half, half]
    out_specs = [row(MIX_W)] * 4 + [row(HEAD_W)] + [row(MIX_W)] * 8
    out_specs[6] = out_specs[8] = head_rows
    if v_transposed_seq is not None:
        tiles = v_transposed_seq // tm
        out_shape[9] = jax.ShapeDtypeStruct((m // v_transposed_seq, MIX_W, v_transposed_seq), BF16)
        out_specs[9] = pl.BlockSpec((1, MIX_W, tm), lambda i: (i // tiles, 0, i % tiles))
    return pl.pallas_call(
        _in_even_kernel,
        grid=(m // tm,),
        in_specs=[
            row(D_MODEL),
            _const_spec((1, D_MODEL)),
            _const_spec((D_MODEL, _E_MAIN)),
            _const_spec((D_MODEL, HEAD_W)),
            _const_spec((1, HEAD_W)),
            _const_spec((MIX_W, MIX_W)),
            _const_spec((1, MIX_W)),
            _const_spec((1, MIX_W)),
            _const_spec((1, HEAD_W)),
        ],
        out_specs=out_specs,
        out_shape=out_shape,
        compiler_params=_params(("arbitrary",)),
        name="in_even",
    )(x2d, norm_g, w_main, w_gate, gate_bias, grp, qn, kn, xqg)


def _mlstm_kernel(q_ref, k_ref, v_ref, gt_ref, ga_ref, ng_ref, s0_ref, m0_ref, o_ref, s_ref, m_ref, *, chunk):
    c = pl.program_id(1)

    @pl.when(c == 0)
    def _():
        s_ref[...] = s0_ref[...]
        m_ref[...] = m0_ref[...]

    src = lax.broadcasted_iota(jnp.int32, (chunk, chunk), 0)
    tgt = lax.broadcasted_iota(jnp.int32, (chunk, chunk), 1)
    causal = src <= tgt
    gt = gt_ref[0]
    gt_t = gt.T
    cum = jnp.dot((tgt <= src).astype(F32), gt, preferred_element_type=F32, precision=lax.Precision.HIGHEST)
    cum_t = jnp.dot(gt_t, causal.astype(F32), preferred_element_type=F32, precision=lax.Precision.HIGHEST)
    ones_rows = jnp.ones((ACC_ROWS - HEAD_W, chunk), BF16)
    for h in range(N_HEADS):
        sl = slice(h * HEAD_W, (h + 1) * HEAD_W)
        b_row = cum_t[N_HEADS + h:N_HEADS + h + 1, :]
        i_row = gt_t[h:h + 1, :]
        c_col = gt[:, h:h + 1] - cum[:, N_HEADS + h:N_HEADS + h + 1]
        state = s_ref[0, h]
        g_row = b_row + m_ref[0, h][:, 0:1]
        logw = jnp.where(causal, c_col + b_row, NEG_INF)
        m_row = jnp.maximum(g_row, jnp.max(logw, axis=0, keepdims=True))
        w_intra = jnp.exp(logw - m_row)
        w_inter = jnp.exp(g_row - m_row)
        q = q_ref[0, :, sl]
        k = k_ref[0, :, sl]
        v_aug = jnp.concatenate([v_ref[0, :, sl].astype(F32).T.astype(BF16), ones_rows], axis=0)
        sc = (_dot_nt(k, q) * w_intra).astype(BF16)
        inter = _dot_nt(state.astype(BF16), q)
        intra = _dot(v_aug, sc)
        num = w_inter * inter[:HEAD_W] + intra[:HEAD_W]
        den = w_inter * inter[HEAD_W:HEAD_W + 1] + intra[HEAD_W:HEAD_W + 1]
        hid = num / jnp.maximum(jnp.abs(den), jnp.exp(-m_row))
        hid = hid * lax.rsqrt(jnp.mean(hid * hid, axis=0, keepdims=True) + EPS)
        o_ref[0, :, sl] = (hid.T * ng_ref[:, sl] * ga_ref[0, :, sl].astype(F32)).astype(BF16)
        m_last = m_row[:, chunk - 1:chunk]
        decay = jnp.exp(g_row[:, chunk - 1:chunk] - m_last)
        w_end = jnp.exp(b_row[:, chunk - 1:chunk] - b_row + i_row - m_last)
        s_ref[0, h] = decay * state + _dot((v_aug.astype(F32) * w_end).astype(BF16), k)
        m_ref[0, h] = jnp.broadcast_to(m_last, (1, HEAD_W))


def _mlstm(q, k, v, gates, gate_a, norm_g, state0, m0, chunk):
    bsz, seq, _ = q.shape
    chunk = min(chunk, seq)
    rows = 1
    tok = lambda width: pl.BlockSpec((rows, chunk, width), lambda b, c: (b, c, 0))
    st_spec = pl.BlockSpec((rows, N_HEADS, ACC_ROWS, HEAD_W), lambda b, c: (b, 0, 0, 0))
    m_spec = pl.BlockSpec((rows, N_HEADS, 1, HEAD_W), lambda b, c: (b, 0, 0, 0))
    return pl.pallas_call(
        functools.partial(_mlstm_kernel, chunk=chunk),
        grid=(bsz // rows, seq // chunk),
        in_specs=[tok(MIX_W), tok(MIX_W), tok(MIX_W), tok(HEAD_W), tok(MIX_W),
                  _const_spec((1, MIX_W)), st_spec, m_spec],
        out_specs=[tok(MIX_W), st_spec, m_spec],
        out_shape=[jax.ShapeDtypeStruct((bsz, seq, MIX_W), BF16),
                   jax.ShapeDtypeStruct(state0.shape, F32),
                   jax.ShapeDtypeStruct(m0.shape, F32)],
        compiler_params=_params(("arbitrary", "arbitrary")),
        name="mlstm",
    )(q, k, v, gates, gate_a, norm_g, state0, m0)


def _split_maps(q):
    lane = lax.broadcasted_iota(jnp.int32, q.shape, 1)
    zero = jnp.zeros_like(q)
    return jnp.concatenate([jnp.where(lane < DK_B, q, zero), jnp.where(lane >= DK_B, q, zero)], axis=0)


def _diff_lambda(lam_refs, lam_init):
    lq1, lk1, lq2, lk2 = (r[...] for r in lam_refs)
    return (jnp.exp(jnp.sum(lq1 * lk1, axis=-1, keepdims=True))
            - jnp.exp(jnp.sum(lq2 * lk2, axis=-1, keepdims=True)) + lam_init)


def _diff_gate(o, lam_init, sg, gate):
    return (_rms_rows(o, sg) * (1.0 - lam_init) * gate.astype(F32)).astype(BF16)


def _diff_exact_kernel(q_ref, k_ref, vt_ref, gate_ref, bias_ref, lq1_ref, lk1_ref, lq2_ref, lk2_ref,
                       sg_ref, o_ref, m_sc, acc_sc, *, tq, heads, lam_init):
    i = pl.program_id(2)
    tiles_per_map = tq // HEAD_W
    m_sc[...] = jnp.full(m_sc.shape, NEG_INF, F32)
    acc_sc[...] = jnp.zeros(acc_sc.shape, F32)
    ones_rows = jnp.ones((ACC_ROWS - HEAD_W, tq), BF16)
    qq = [_split_maps(q_ref[0, :, h * HEAD_W:(h + 1) * HEAD_W]) for h in range(heads)]

    def block(j, table):
        start = pl.multiple_of(j * tq, tq)
        scores = [_dot_nt(k_ref[0, pl.ds(start, tq), h * HEAD_W:(h + 1) * HEAD_W], qq[h]) for h in range(heads)]
        for h in range(heads):
            hs = slice(h * HEAD_W, (h + 1) * HEAD_W)
            offset = bias_ref[h, 0, 1:2, 0:1] * ((j - i) * tq).astype(F32)
            v_aug = jnp.concatenate([vt_ref[0, hs, pl.ds(start, tq)], ones_rows], axis=0)
            m_old = m_sc[h]
            m_new, p = [], []
            for t in range(2 * tiles_per_map):
                cs = slice(t * HEAD_W, (t + 1) * HEAD_W)
                part = (t % tiles_per_map) * HEAD_W
                s = scores[h][:, cs] + bias_ref[h, table, :, part:part + HEAD_W]
                m_t = jnp.maximum(m_old[:, cs], jnp.max(s, axis=0, keepdims=True) + offset)
                p.append(jnp.exp2(s - (m_t - offset)).astype(BF16))
                m_new.append(m_t)
            m_new = jnp.concatenate(m_new, axis=-1)
            acc_sc[h] = jnp.exp2(m_old - m_new) * acc_sc[h] + _dot(v_aug, jnp.concatenate(p, axis=-1))
            m_sc[h] = m_new

    def earlier(j, carry):
        block(j, 0)
        return carry

    lax.fori_loop(0, i, earlier, 0)
    block(i, 1)
    _diff_finish(acc_sc, tq, heads, (lq1_ref, lk1_ref, lq2_ref, lk2_ref), lam_init, sg_ref, gate_ref, o_ref)


def _diff_fixed_kernel(q_ref, qpos_ref, k_ref, kpos_ref, vt_ref, gate_ref, corr_ref, lq1_ref, lk1_ref, lq2_ref,
                       lk2_ref, sg_ref, o_ref, acc_sc, *, tq, heads, lam_init):
    i = pl.program_id(2)
    tiles_per_map = tq // HEAD_W
    acc_sc[...] = jnp.zeros(acc_sc.shape, F32)
    ones_rows = jnp.ones((ACC_ROWS - HEAD_W, tq), BF16)
    qq = [jnp.concatenate([_split_maps(q_ref[0, :, h * HEAD_W:(h + 1) * HEAD_W]),
                           jnp.concatenate([qpos_ref[h], qpos_ref[h]], axis=0)], axis=-1)
          for h in range(heads)]

    def block(j, diagonal):
        start = pl.multiple_of(j * tq, tq)
        for h in range(heads):
            hs = slice(h * HEAD_W, (h + 1) * HEAD_W)
            k_aug = jnp.concatenate([k_ref[0, pl.ds(start, tq), hs], kpos_ref[h, pl.ds(start, tq), :]], axis=-1)
            s = _dot_nt(k_aug, qq[h])
            if diagonal:
                s = jnp.concatenate(
                    [s[:, t * HEAD_W:(t + 1) * HEAD_W]
                     + corr_ref[h, :, (t % tiles_per_map) * HEAD_W:(t % tiles_per_map + 1) * HEAD_W]
                     for t in range(2 * tiles_per_map)], axis=-1)
            v_aug = jnp.concatenate([vt_ref[0, hs, pl.ds(start, tq)], ones_rows], axis=0)
            acc_sc[h] = acc_sc[h] + _dot(v_aug, jnp.exp2(s).astype(BF16))

    def earlier(j, carry):
        block(j, False)
        return carry

    lax.fori_loop(0, i, earlier, 0)
    block(i, True)
    _diff_finish(acc_sc, tq, heads, (lq1_ref, lk1_ref, lq2_ref, lk2_ref), lam_init, sg_ref, gate_ref, o_ref)


def _diff_finish(acc_sc, tq, heads, lam_refs, lam_init, sg_ref, gate_ref, o_ref):
    lam = _diff_lambda(lam_refs, lam_init)
    for h in range(heads):
        hs = slice(h * HEAD_W, (h + 1) * HEAD_W)
        acc = acc_sc[h]
        o_t = (acc[:HEAD_W, :tq] / acc[HEAD_W:HEAD_W + 1, :tq]
               - lam * (acc[:HEAD_W, tq:] / acc[HEAD_W:HEAD_W + 1, tq:]))
        o_ref[0, :, hs] = _diff_gate(o_t.T, lam_init, sg_ref[...], gate_ref[0, :, hs])


def _alibi_tables(tq):
    slopes = jnp.array(ALIBI_SLOPES, F32) * LOG2E
    kk = jnp.arange(tq, dtype=jnp.int32)[:, None]
    qi = jnp.arange(tq, dtype=jnp.int32)[None, :]
    base = slopes[:, None, None] * jnp.broadcast_to(kk.astype(F32), (tq, tq))[None]
    diag = slopes[:, None, None] * (qi - jnp.abs(qi - kk)).astype(F32)[None]
    diag = jnp.where((kk // CHUNK <= qi // CHUNK)[None], diag, NEG_INF)
    return jnp.stack([base, diag], axis=1)


def _truncate_to_bf16_grid(x):
    bits = lax.bitcast_convert_type(x, jnp.uint32) & jnp.uint32(0xFFFF0000)
    return lax.bitcast_convert_type(bits, F32)


def _split3(x):
    hi = _truncate_to_bf16_grid(x)
    mid = _truncate_to_bf16_grid(x - hi)
    return hi.astype(BF16), mid.astype(BF16), (x - hi - mid).astype(BF16)


def _position_tables(seq, tq, ub):
    slopes = jnp.array(ALIBI_SLOPES, F32)[:, None] * LOG2E
    pos = jnp.arange(seq, dtype=F32)[None, :]
    ones = jnp.ones((N_HEADS, seq), BF16)
    lanes = lambda cols: jnp.pad(jnp.stack(cols, axis=-1), ((0, 0), (0, 0), (0, HEAD_W - len(cols))))
    kpos = lanes([*_split3(slopes * pos), ones, ones, ones])
    qpos = lanes([ones, ones, ones, *_split3(-slopes * pos - ub)])
    kk = jnp.arange(tq, dtype=jnp.int32)[:, None]
    qi = jnp.arange(tq, dtype=jnp.int32)[None, :]
    corr = slopes[:, :, None] * (-2.0 * jnp.maximum(kk - qi, 0).astype(F32))[None]
    corr = jnp.where((kk // CHUNK <= qi // CHUNK)[None], corr, NEG_INF)
    return kpos, qpos, corr


def _diff_prompt(q, k, vt, gate, lam_vecs, subln_g, lam_init, tq, heads, ub):
    bsz, seq, _ = q.shape
    tq = min(tq, seq)
    width = heads * HEAD_W
    grid = (bsz, N_HEADS // heads, seq // tq)
    q_spec = pl.BlockSpec((1, tq, width), lambda b, h, i: (b, i, h))
    once = dict(pipeline_mode=pl.Buffered(1))
    k_spec = pl.BlockSpec((1, seq, width), lambda b, h, i: (b, 0, h), **once)
    vt_spec = pl.BlockSpec((1, width, seq), lambda b, h, i: (b, h, 0), **once)
    tail_specs = [_const_spec((1, DK_B))] * 4 + [_const_spec((1, HEAD_W))]
    out_shape = jax.ShapeDtypeStruct((bsz, seq, MIX_W), BF16)
    acc_buf = pltpu.VMEM((heads, ACC_ROWS, 2 * tq), F32)
    sem = _params(("arbitrary", "arbitrary", "arbitrary"))

    def fixed():
        kpos, qpos, corr = _position_tables(seq, tq, ub)
        return pl.pallas_call(
            functools.partial(_diff_fixed_kernel, tq=tq, heads=heads, lam_init=lam_init),
            grid=grid,
            in_specs=[q_spec,
                      pl.BlockSpec((heads, tq, HEAD_W), lambda b, h, i: (h, i, 0)),
                      k_spec,
                      pl.BlockSpec((heads, seq, HEAD_W), lambda b, h, i: (h, 0, 0), **once),
                      vt_spec, q_spec,
                      pl.BlockSpec((heads, tq, tq), lambda b, h, i: (h, 0, 0), **once)] + tail_specs,
            out_specs=q_spec, out_shape=out_shape, scratch_shapes=[acc_buf],
            compiler_params=sem, name="diff_fixed",
        )(q, qpos, k, kpos, vt, gate, corr, *lam_vecs, subln_g)

    def exact():
        return pl.pallas_call(
            functools.partial(_diff_exact_kernel, tq=tq, heads=heads, lam_init=lam_init),
            grid=grid,
            in_specs=[q_spec, k_spec, vt_spec, q_spec,
                      pl.BlockSpec((heads, 2, tq, tq), lambda b, h, i: (h, 0, 0, 0), **once)] + tail_specs,
            out_specs=q_spec, out_shape=out_shape,
            scratch_shapes=[pltpu.VMEM((heads, 1, 2 * tq), F32), acc_buf],
            compiler_params=sem, name="diff_exact",
        )(q, k, vt, gate, _alibi_tables(tq), *lam_vecs, subln_g)

    return lax.cond(ub <= FIXED_SHIFT_MAX, fixed, exact)


def _diff_sample_kernel(q_ref, k_ref, v_ref, kp_ref, vp_ref, gate_ref, slope_ref, lq1_ref, lk1_ref, lq2_ref,
                        lk2_ref, sg_ref, o_ref, *, lam_init):
    rows = q_ref.shape[1]
    past = kp_ref.shape[1] // N_HEADS
    lam = _diff_lambda((lq1_ref, lk1_ref, lq2_ref, lk2_ref), lam_init)
    r = lax.broadcasted_iota(jnp.int32, (2 * rows, 1), 0)
    r = jnp.where(r >= rows, r - rows, r)
    dist_past = (past + r - lax.broadcasted_iota(jnp.int32, (2 * rows, past), 1)).astype(F32)
    dist_new = jnp.abs(r - lax.broadcasted_iota(jnp.int32, (2 * rows, rows), 1)).astype(F32)
    for h in range(N_HEADS):
        hs = slice(h * HEAD_W, (h + 1) * HEAD_W)
        slope = slope_ref[h][:, 0:1]
        qq = _split_maps(q_ref[0, :, hs])
        k_past = kp_ref[0, pl.ds(h, past, stride=N_HEADS), :].astype(BF16)
        v_past = vp_ref[0, pl.ds(h, past, stride=N_HEADS), :].astype(BF16)
        s_past = _dot_nt(qq, k_past) - slope * dist_past
        s_new = _dot_nt(qq, k_ref[0, :, hs]) - slope * dist_new
        m = jnp.maximum(jnp.max(s_past, axis=-1, keepdims=True), jnp.max(s_new, axis=-1, keepdims=True))
        acc = (_dot(jnp.exp2(s_past - m).astype(BF16), jnp.concatenate([v_past, _ones_col(past)], axis=-1))
               + _dot(jnp.exp2(s_new - m).astype(BF16), jnp.concatenate([v_ref[0, :, hs], _ones_col(rows)], axis=-1)))
        o = (acc[:rows, :HEAD_W] / acc[:rows, HEAD_W:HEAD_W + 1]
             - lam * (acc[rows:, :HEAD_W] / acc[rows:, HEAD_W:HEAD_W + 1]))
        o_ref[0, :, hs] = _diff_gate(o, lam_init, sg_ref[...], gate_ref[0, :, hs])


def _diff_sample(q, k_new, v_new, k_past, v_past, gate, slopes, lam_vecs, subln_g, lam_init):
    bsz, rows, _ = q.shape
    past = k_past.shape[1]
    tok = pl.BlockSpec((1, rows, MIX_W), lambda b: (b, 0, 0))
    cache = pl.BlockSpec((1, past * N_HEADS, HEAD_W), lambda b: (b, 0, 0))
    head_rows = lambda a: a.reshape(bsz, past * N_HEADS, HEAD_W)
    return pl.pallas_call(
        functools.partial(_diff_sample_kernel, lam_init=lam_init),
        grid=(bsz,),
        in_specs=[tok, tok, tok, cache, cache, tok, _const_spec((N_HEADS, 1, HEAD_W))]
                 + [_const_spec((1, DK_B))] * 4 + [_const_spec((1, HEAD_W))],
        out_specs=tok,
        out_shape=jax.ShapeDtypeStruct((bsz, rows, MIX_W), BF16),
        compiler_params=_params(("arbitrary",)),
        name="diff_sample",
    )(q, k_new, v_new, head_rows(k_past), head_rows(v_past), gate, slopes, *lam_vecs, subln_g)


def _cross_attn(xq, gate, mem_k, mem_v):
    outs = []
    for h in range(N_HEADS):
        sl = slice(h * HEAD_W, (h + 1) * HEAD_W)
        s = _dot_nt(xq[:, sl], mem_k[:, sl])
        p = jnp.exp2(s - jnp.max(s, axis=-1, keepdims=True))
        outs.append(_dot(p.astype(BF16), mem_v[:, sl]) / jnp.sum(p, axis=-1, keepdims=True))
    return (jnp.concatenate(outs, axis=-1) * gate).astype(BF16)


def _out_even_kernel(x_ref, oa_ref, ob_ref, xq_ref, gx_ref, mk_ref, mv_ref, w_ref, y_ref):
    ox = _cross_attn(xq_ref[0], gx_ref[0].astype(F32), mk_ref[0], mv_ref[0])
    mixed = jnp.concatenate([oa_ref[0], ob_ref[0], ox], axis=-1)
    y_ref[0] = x_ref[0] + _dot(mixed, w_ref[...])


def _out_even(x, oa, ob, xq, gx, mem_k, mem_v, w_out, tm):
    bsz, seq, _ = x.shape
    tm = min(tm, seq)
    tok = lambda width: pl.BlockSpec((1, tm, width), lambda b, i: (b, i, 0))
    mem_spec = pl.BlockSpec((1, N_MEM, MIX_W), lambda b, i: (b, 0, 0))
    return pl.pallas_call(
        _out_even_kernel,
        grid=(bsz, seq // tm),
        in_specs=[tok(D_MODEL), tok(MIX_W), tok(MIX_W), tok(MIX_W), tok(MIX_W), mem_spec, mem_spec,
                  _const_spec((3 * MIX_W, D_MODEL))],
        out_specs=tok(D_MODEL),
        out_shape=jax.ShapeDtypeStruct(x.shape, F32),
        compiler_params=_params(("arbitrary", "arbitrary")),
        name="out_even",
    )(x, oa, ob, xq, gx, mem_k, mem_v, w_out)


_O_CU, _O_CG, _O_CZ = 0, D_MODEL, 2 * D_MODEL
_O_XQ, _O_XZ = 3 * D_MODEL, 3 * D_MODEL + MIX_W


def _odd_kernel(x_ref, hist_ref, g_ref, w_in_ref, cw_ref, cb_ref, lg_ref, lb_ref, xqg_ref, mk_ref, mv_ref,
                w_out_ref, y_ref, tail_ref, u_sc, conv_sc, *, tm):
    i = pl.program_id(1)

    @pl.when(i == 0)
    def _():
        u_sc[0:CONV_HIST, :] = hist_ref[0]

    x = x_ref[0]
    xn = _rms_rows(x, g_ref[...]).astype(BF16)

    def proj(c0, width):
        return _dot(xn, w_in_ref[:, c0:c0 + width])

    u = proj(_O_CU, D_MODEL) * _sigmoid(proj(_O_CG, D_MODEL))
    u_sc[CONV_HIST:CONV_HIST + tm, :] = u
    rows = min(CONV_ROWS, tm)
    for c0 in range(0, D_MODEL, HEAD_W):
        cs = slice(c0, c0 + HEAD_W)
        for r0 in range(0, tm, rows):
            acc = jnp.broadcast_to(cb_ref[:, cs], (rows, HEAD_W))
            span = rows + CONV_HIST
            aligned = u_sc[r0:r0 + span, cs]
            for phase in range(SUBLANES):
                offs = [o for o in range(CONV_HIST - (CONV_W - 1), CONV_HIST + 1) if o % SUBLANES == phase]
                win = pltpu.roll(aligned, span - phase, axis=0) if phase else aligned
                for off in offs:
                    j = off - (CONV_HIST - (CONV_W - 1))
                    acc = acc + cw_ref[j:j + 1, cs] * win[off - phase:off - phase + rows]
            conv_sc[r0:r0 + rows, cs] = acc
    acc = conv_sc[...]
    mu = jnp.mean(acc, axis=-1, keepdims=True)
    cen = acc - mu
    var = jnp.mean(cen * cen, axis=-1, keepdims=True)
    c = _silu(cen * lax.rsqrt(var + EPS) * lg_ref[...] + lb_ref[...])
    c = (c * _silu(proj(_O_CZ, D_MODEL))).astype(BF16)
    xq = proj(_O_XQ, MIX_W)
    xqg = xqg_ref[...] * ((HEAD_W ** -0.5) * LOG2E)
    xq = jnp.concatenate([_rms_rows(xq[:, h * HEAD_W:(h + 1) * HEAD_W], xqg) for h in range(N_HEADS)], axis=-1)
    ox = _cross_attn(xq.astype(BF16), _silu(proj(_O_XZ, MIX_W)), mk_ref[0], mv_ref[0])
    y_ref[0] = x + _dot(jnp.concatenate([c, ox], axis=-1), w_out_ref[...])
    tail = u_sc[tm:tm + CONV_HIST, :]
    u_sc[0:CONV_HIST, :] = tail
    tail_ref[0] = tail


def _odd_layer(x, hist, norm_g, w_in, conv_w, conv_b, ln_g, ln_b, xqg, mem_k, mem_v, w_out, tm):
    bsz, seq, _ = x.shape
    tm = min(tm, seq)
    tok = pl.BlockSpec((1, tm, D_MODEL), lambda b, i: (b, i, 0))
    hist_spec = pl.BlockSpec((1, CONV_HIST, D_MODEL), lambda b, i: (b, 0, 0))
    mem_spec = pl.BlockSpec((1, N_MEM, MIX_W), lambda b, i: (b, 0, 0))
    vec = _const_spec((1, D_MODEL))
    return pl.pallas_call(
        functools.partial(_odd_kernel, tm=tm),
        grid=(bsz, seq // tm),
        in_specs=[tok, hist_spec, vec, _const_spec(w_in.shape), _const_spec(conv_w.shape), vec, vec, vec,
                  _const_spec((1, HEAD_W)), mem_spec, mem_spec, _const_spec(w_out.shape)],
        out_specs=[tok, hist_spec],
        out_shape=[jax.ShapeDtypeStruct(x.shape, F32), jax.ShapeDtypeStruct((bsz, CONV_HIST, D_MODEL), F32)],
        scratch_shapes=[pltpu.VMEM((CONV_HIST + tm, D_MODEL), F32), pltpu.VMEM((tm, D_MODEL), F32)],
        compiler_params=_params(("arbitrary", "arbitrary")),
        name="odd_layer",
    )(x, hist, norm_g, w_in, conv_w, conv_b, ln_g, ln_b, xqg, mem_k, mem_v, w_out)


def _even_layer(x, mem_k, mem_v, hist, wts, lam_init, tm_in, tm_out, chunk, tq):
    (norm_g, w_main, w_gate, gate_bias, grp, qn, kn, xqg, mlstm_g, slopes, lam_vecs, subln_g, w_out) = wts
    bsz, seq, _ = x.shape
    (qa, ka, va, ga, gates, qb, kb, kbh, vb, vbh, gb, xq, gx) = _in_even(
        x.reshape(bsz * seq, D_MODEL), norm_g, w_main, w_gate, gate_bias, grp, qn, kn, xqg, tm_in,
        v_transposed_seq=seq if hist is None else None)
    tok = lambda a: a.reshape(bsz, seq, a.shape[-1])
    if hist is None:
        state0 = jnp.zeros((bsz, N_HEADS, ACC_ROWS, HEAD_W), F32)
        m0 = jnp.zeros((bsz, N_HEADS, 1, HEAD_W), F32)
    else:
        k_past, v_past, c0, n0, m0 = hist
        state0 = jnp.concatenate([jnp.swapaxes(c0, -1, -2), n0[..., None, :],
                                  jnp.zeros((bsz, N_HEADS, ACC_ROWS - HEAD_W - 1, HEAD_W), F32)], axis=-2)
        m0 = jnp.broadcast_to(m0[..., None, None], (bsz, N_HEADS, 1, HEAD_W))
    oa, state1, m1 = _mlstm(tok(qa), tok(ka), tok(va), tok(gates), tok(ga), mlstm_g, state0, m0, chunk)
    if hist is None:
        ub = (8.0 * LOG2E * 1.01) * jnp.max(jnp.abs(qn)) * jnp.max(jnp.abs(kn))
        ob = _diff_prompt(tok(qb), tok(kbh), vbh, tok(gb), lam_vecs, subln_g, lam_init, tq, heads=4, ub=ub)
    else:
        ob = _diff_sample(tok(qb), tok(kbh), tok(vbh), k_past, v_past, tok(gb), slopes, lam_vecs, subln_g,
                          lam_init)
    y = _out_even(x, oa, ob, tok(xq), tok(gx), mem_k, mem_v, w_out, tm_out)
    return (y, kb.reshape(bsz, seq, N_HEADS, HEAD_W), vb.reshape(bsz, seq, N_HEADS, HEAD_W),
            jnp.swapaxes(state1[..., :HEAD_W, :], -1, -2), state1[..., HEAD_W, :], m1[:, :, 0, 0])


def kernel(x_prompt, x_sample, mem_prompt, cache_xk, cache_xv, cache_k, cache_v, state_C, state_n, state_m,
           state_conv, norm_g, w_in_a, b_ig, b_fg, mlstm_norm_g, qn_g, kn_g, lam_q1, lam_k1, lam_q2, lam_k2,
           subln_g, w_out_a, w_in_c, conv_w, conv_b, conv_ln_g, conv_ln_b, w_out_c, mem_norm_g, w_mem_kv,
           xq_norm_g, xk_norm_g):
    depth = norm_g.shape[0]
    bsz = x_prompt.shape[0]
    dec = x_sample.shape[0]
    p_xk, p_xv = _mem_kv(mem_prompt, mem_norm_g, w_mem_kv, xk_norm_g)
    mem_k_p, mem_v_p = p_xk.astype(BF16), p_xv.astype(BF16)
    mem_k_s = cache_xk.reshape(depth, dec, N_MEM, MIX_W).astype(BF16)
    mem_v_s = cache_xv.reshape(depth, dec, N_MEM, MIX_W).astype(BF16)

    lane_grp = jnp.arange(MIX_W) // DK_B
    grp = (lane_grp[:, None] == lane_grp[None, :]).astype(BF16)
    slopes = jnp.broadcast_to(jnp.array(ALIBI_SLOPES, F32)[:, None, None] * LOG2E, (N_HEADS, 1, HEAD_W))
    n_gate = 2 * N_HEADS
    gate0 = 5 * MIX_W

    yp, ys = x_prompt, x_sample
    outs = {name: [] for name in ("p_k", "p_v", "p_C", "p_n", "p_m", "p_conv",
                                  "s_k", "s_v", "s_C", "s_n", "s_m", "s_conv")}
    for layer in range(depth):
        if layer % 2 == 0:
            e = layer // 2
            w = w_in_a[e]
            w_main = jnp.concatenate([w[:, :gate0], w[:, gate0 + n_gate:]], axis=1).astype(BF16)
            w_gate = jnp.pad(w[:, gate0:gate0 + n_gate], ((0, 0), (0, HEAD_W - n_gate))).astype(BF16)
            gate_bias = jnp.pad(jnp.concatenate([b_ig[e], b_fg[e]]), (0, HEAD_W - n_gate)).reshape(1, HEAD_W)
            wts = (norm_g[layer].reshape(1, D_MODEL), w_main, w_gate, gate_bias, grp,
                   jnp.tile(qn_g[e], MIX_W // DK_B).reshape(1, MIX_W),
                   jnp.tile(kn_g[e], MIX_W // DK_B).reshape(1, MIX_W),
                   xq_norm_g[layer].reshape(1, HEAD_W), mlstm_norm_g[e].reshape(1, MIX_W), slopes,
                   tuple(v[e].reshape(1, DK_B) for v in (lam_q1, lam_k1, lam_q2, lam_k2)),
                   subln_g[e].reshape(1, HEAD_W), w_out_a[e].astype(BF16))
            lam_init = _lambda_init(layer)
            yp, k_new, v_new, c1, n1, m1 = _even_layer(yp, mem_k_p[layer], mem_v_p[layer], None, wts, lam_init,
                                                       tm_in=256, tm_out=512, chunk=256, tq=512)
            for name, val in zip(("p_k", "p_v", "p_C", "p_n", "p_m"), (k_new, v_new, c1, n1, m1)):
                outs[name].append(val)
            hist = (cache_k[e], cache_v[e], state_C[e], state_n[e], state_m[e])
            ys, k_new, v_new, c1, n1, m1 = _even_layer(ys, mem_k_s[layer], mem_v_s[layer], hist, wts, lam_init,
                                                       tm_in=256, tm_out=512, chunk=256, tq=512)
            for name, val in zip(("s_k", "s_v", "s_C", "s_n", "s_m"), (k_new, v_new, c1, n1, m1)):
                outs[name].append(val)
        else:
            o = layer // 2
            vec = lambda a: a.reshape(1, -1)
            wts = (vec(norm_g[layer]), w_in_c[o].astype(BF16), jnp.pad(conv_w[o], ((0, 1), (0, 0))),
                   vec(conv_b[o]), vec(conv_ln_g[o]), vec(conv_ln_b[o]), vec(xq_norm_g[layer]))
            w_out = w_out_c[o].astype(BF16)
            pad = CONV_HIST - (CONV_W - 1)
            zero_hist = jnp.zeros((bsz, CONV_HIST, D_MODEL), F32)
            yp, tail = _odd_layer(yp, zero_hist, *wts, mem_k_p[layer], mem_v_p[layer], w_out, tm=256)
            outs["p_conv"].append(tail[:, pad:])
            hist = jnp.pad(state_conv[o], ((0, 0), (pad, 0), (0, 0)))
            ys, tail = _odd_layer(ys, hist, *wts, mem_k_s[layer], mem_v_s[layer], w_out, tm=256)
            outs["s_conv"].append(tail[:, pad:])

    head5 = lambda a: a.reshape(a.shape[:-1] + (N_HEADS, HEAD_W))
    st = {name: jnp.stack(vals) for name, vals in outs.items()}
    return (yp, ys, head5(p_xk), head5(p_xv), st["p_k"], st["p_v"], st["p_C"], st["p_n"], st["p_m"], st["p_conv"],
            st["s_k"], st["s_v"], st["s_C"], st["s_n"], st["s_m"], st["s_conv"])
```

```python
import functools
import math

import jax
import jax.numpy as jnp
from jax import lax
from jax.experimental import pallas as pl
from jax.experimental.pallas import tpu as pltpu

F32 = jnp.float32
BF16 = jnp.bfloat16

D_MODEL = 1024
CHUNK = 64
EPS = 1e-6
N_HEADS = 4
HEAD_W = 128
MIX_W = N_HEADS * HEAD_W
DK_B = 64
CONV_W = 31
CONV_HIST = 32
CONV_ROWS = 128
SUBLANES = 8
N_MEM = 256
ALIBI_SLOPES = tuple(2.0 ** (-8.0 * (h + 1) / N_HEADS) for h in range(N_HEADS))
VMEM_LIMIT = 56 * 1024 * 1024

NEG_INF = float("-inf")
LOG2E = math.log2(math.e)
FIXED_SHIFT_MAX = 32.0
ACC_ROWS = HEAD_W + 16


def _lambda_init(layer):
    return 0.8 - 0.6 * math.exp(-0.3 * layer)


def _params(sem):
    return pltpu.CompilerParams(dimension_semantics=sem, vmem_limit_bytes=VMEM_LIMIT)


def _const_spec(shape):
    zeros = (0,) * len(shape)
    return pl.BlockSpec(shape, lambda *_: zeros)


def _rms_rows(x, g):
    return x * lax.rsqrt(jnp.mean(x * x, axis=-1, keepdims=True) + EPS) * g


def _sigmoid(x):
    return 1.0 / (1.0 + jnp.exp2(x * (-LOG2E)))


def _silu(x):
    return x * _sigmoid(x)


def _dot(a, b):
    return jnp.dot(a, b, preferred_element_type=F32)


def _dot_nt(a, b):
    return lax.dot_general(a, b, (((1,), (1,)), ((), ())), preferred_element_type=F32)


def _dot_tn(a, b):
    return lax.dot_general(a, b, (((0,), (0,)), ((), ())), preferred_element_type=F32)


def _ones_col(rows):
    lane = lax.broadcasted_iota(jnp.int32, (rows, HEAD_W), 1)
    return jnp.where(lane == 0, 1.0, 0.0).astype(BF16)


def _mem_kv_kernel(mem_ref, g_ref, w_ref, kg_ref, k_ref, v_ref):
    xn = _rms_rows(mem_ref[0], g_ref[0]).astype(BF16)
    kv = _dot(xn, w_ref[0])
    kg = kg_ref[0]
    for h in range(N_HEADS):
        sl = slice(h * HEAD_W, (h + 1) * HEAD_W)
        k_ref[0, 0, :, sl] = _rms_rows(kv[:, sl], kg)
    v_ref[0, 0] = kv[:, MIX_W:]


def _mem_kv(mem, g, w_kv, kg):
    depth = w_kv.shape[0]
    bsz, n_mem, _ = mem.shape
    out = jax.ShapeDtypeStruct((depth, bsz, n_mem, MIX_W), F32)
    return pl.pallas_call(
        _mem_kv_kernel,
        grid=(depth, bsz),
        in_specs=[
            pl.BlockSpec((1, n_mem, D_MODEL), lambda l, b: (b, 0, 0)),
            pl.BlockSpec((1, 1, D_MODEL), lambda l, b: (l, 0, 0)),
            pl.BlockSpec((1, D_MODEL, 2 * MIX_W), lambda l, b: (l, 0, 0)),
            pl.BlockSpec((1, 1, HEAD_W), lambda l, b: (l, 0, 0)),
        ],
        out_specs=[pl.BlockSpec((1, 1, n_mem, MIX_W), lambda l, b: (l, b, 0, 0))] * 2,
        out_shape=[out, out],
        compiler_params=_params(("arbitrary", "arbitrary")),
        name="mem_kv",
    )(mem, g.reshape(depth, 1, D_MODEL), w_kv.astype(BF16), kg.reshape(depth, 1, HEAD_W))


_E_AQ, _E_AK, _E_AV, _E_AO, _E_AZ, _E_BQ, _E_BK, _E_BV, _E_BZ, _E_XQ, _E_XZ = (i * MIX_W for i in range(11))
_E_MAIN = 11 * MIX_W


def _group_mean_sq(y, grp):
    sq = y * y
    hi = sq.astype(BF16)
    lo = (sq - hi.astype(F32)).astype(BF16)
    return (_dot(hi, grp) + _dot(lo, grp)) * (1.0 / DK_B)


def _store_head_rows(ref, x):
    tokens = x.shape[0]
    for h in range(N_HEADS):
        ref[pl.ds(h, tokens, stride=N_HEADS), :] = x[:, h * HEAD_W:(h + 1) * HEAD_W]


def _in_even_kernel(x_ref, g_ref, w_ref, wg_ref, gbias_ref, grp_ref, qn_ref, kn_ref, xqg_ref,
                    qa_ref, ka_ref, va_ref, ga_ref, gt_ref, qb_ref, kb_ref, kbh_ref, vb_ref, vbh_ref,
                    gb_ref, xq_ref, gx_ref):
    xn = _rms_rows(x_ref[...], g_ref[...]).astype(BF16)

    def proj(c0):
        return _dot(xn, w_ref[:, c0:c0 + MIX_W])

    qa_ref[...] = proj(_E_AQ).astype(BF16)
    ka_ref[...] = (proj(_E_AK) * (HEAD_W ** -0.5)).astype(BF16)
    va_ref[...] = proj(_E_AV).astype(BF16)
    ga_ref[...] = (_sigmoid(proj(_E_AO)) * _silu(proj(_E_AZ))).astype(BF16)
    gt = _dot(xn, wg_ref[...]) + gbias_ref[...]
    lane = lax.broadcasted_iota(jnp.int32, gt.shape, 1)
    log_sig = jnp.minimum(gt, 0.0) - jnp.log1p(jnp.exp(-jnp.abs(gt)))
    gt_ref[...] = jnp.where((lane >= N_HEADS) & (lane < 2 * N_HEADS), log_sig, gt)
    grp = grp_ref[...]
    bq = proj(_E_BQ)
    q_scale = (DK_B ** -0.5) * LOG2E
    qb_ref[...] = (bq * lax.rsqrt(_group_mean_sq(bq, grp) + EPS) * qn_ref[...] * q_scale).astype(BF16)
    bk = proj(_E_BK)
    kb = bk * lax.rsqrt(_group_mean_sq(bk, grp) + EPS) * kn_ref[...]
    _store_head_rows(kb_ref, kb)
    kbh_ref[...] = kb.astype(BF16)
    bv = proj(_E_BV)
    _store_head_rows(vb_ref, bv)
    if len(vbh_ref.shape) == 3:
        vbh_ref[0] = bv.T.astype(BF16)
    else:
        vbh_ref[...] = bv.astype(BF16)
    gb_ref[...] = _silu(proj(_E_BZ)).astype(BF16)
    xq = proj(_E_XQ)
    xqg = xqg_ref[...] * ((HEAD_W ** -0.5) * LOG2E)
    for h in range(N_HEADS):
        sl = slice(h * HEAD_W, (h + 1) * HEAD_W)
        xq_ref[:, sl] = _rms_rows(xq[:, sl], xqg).astype(BF16)
    gx_ref[...] = _silu(proj(_E_XZ)).astype(BF16)


def _in_even(x2d, norm_g, w_main, w_gate, gate_bias, grp, qn, kn, xqg, tm, v_transposed_seq):
    m = x2d.shape[0]
    tm = min(tm, m)
    row = lambda width: pl.BlockSpec((tm, width), lambda i: (i, 0))
    half = jax.ShapeDtypeStruct((m, MIX_W), BF16)
    full = jax.ShapeDtypeStruct((m * N_HEADS, HEAD_W), F32)
    head_rows = pl.BlockSpec((tm * N_HEADS, HEAD_W), lambda i: (i, 0))
    out_shape = [half, half, half, half, jax.ShapeDtypeStruct((m, HEAD_W), F32),
                 half, full, half, full, half, half, half, half]
    out_specs = [row(MIX_W)] * 4 + [row(HEAD_W)] + [row(MIX_W)] * 8
    out_specs[6] = out_specs[8] = head_rows
    if v_transposed_seq is not None:
        tiles = v_transposed_seq // tm
        out_shape[9] = jax.ShapeDtypeStruct((m // v_transposed_seq, MIX_W, v_transposed_seq), BF16)
        out_specs[9] = pl.BlockSpec((1, MIX_W, tm), lambda i: (i // tiles, 0, i % tiles))
    return pl.pallas_call(
        _in_even_kernel,
        grid=(m // tm,),
        in_specs=[
            row(D_MODEL),
            _const_spec((1, D_MODEL)),
            _const_spec((D_MODEL, _E_MAIN)),
            _const_spec((D_MODEL, HEAD_W)),
            _const_spec((1, HEAD_W)),
            _const_spec((MIX_W, MIX_W)),
            _const_spec((1, MIX_W)),
            _const_spec((1, MIX_W)),
            _const_spec((1, HEAD_W)),
        ],
        out_specs=out_specs,
        out_shape=out_shape,
        compiler_params=_params(("arbitrary",)),
        name="in_even",
    )(x2d, norm_g, w_main, w_gate, gate_bias, grp, qn, kn, xqg)


def _mlstm_kernel(q_ref, k_ref, v_ref, gt_ref, ga_ref, ng_ref, s0_ref, m0_ref, o_ref, s_ref, m_ref, *, chunk):
    c = pl.program_id(1)

    @pl.when(c == 0)
    def _():
        s_ref[...] = s0_ref[...]
        m_ref[...] = m0_ref[...]

    src = lax.broadcasted_iota(jnp.int32, (chunk, chunk), 0)
    tgt = lax.broadcasted_iota(jnp.int32, (chunk, chunk), 1)
    causal = src <= tgt
    gt = gt_ref[0]
    gt_t = gt.T
    cum = jnp.dot((tgt <= src).astype(F32), gt, preferred_element_type=F32, precision=lax.Precision.HIGHEST)
    cum_t = jnp.dot(gt_t, causal.astype(F32), preferred_element_type=F32, precision=lax.Precision.HIGHEST)
    ones_rows = jnp.ones((ACC_ROWS - HEAD_W, chunk), BF16)
    for h in range(N_HEADS):
        sl = slice(h * HEAD_W, (h + 1) * HEAD_W)
        b_row = cum_t[N_HEADS + h:N_HEADS + h + 1, :]
        i_row = gt_t[h:h + 1, :]
        c_col = gt[:, h:h + 1] - cum[:, N_HEADS + h:N_HEADS + h + 1]
        state = s_ref[0, h]
        g_row = b_row + m_ref[0, h][:, 0:1]
        logw = jnp.where(causal, c_col + b_row, NEG_INF)
        m_row = jnp.maximum(g_row, jnp.max(logw, axis=0, keepdims=True))
        w_intra = jnp.exp(logw - m_row)
        w_inter = jnp.exp(g_row - m_row)
        q = q_ref[0, :, sl]
        k = k_ref[0, :, sl]
        v_aug = jnp.concatenate([v_ref[0, :, sl].astype(F32).T.astype(BF16), ones_rows], axis=0)
        sc = (_dot_nt(k, q) * w_intra).astype(BF16)
        inter = _dot_nt(state.astype(BF16), q)
        intra = _dot(v_aug, sc)
        num = w_inter * inter[:HEAD_W] + intra[:HEAD_W]
        den = w_inter * inter[HEAD_W:HEAD_W + 1] + intra[HEAD_W:HEAD_W + 1]
        hid = num / jnp.maximum(jnp.abs(den), jnp.exp(-m_row))
        hid = hid * lax.rsqrt(jnp.mean(hid * hid, axis=0, keepdims=True) + EPS)
        o_ref[0, :, sl] = (hid.T * ng_ref[:, sl] * ga_ref[0, :, sl].astype(F32)).astype(BF16)
        m_last = m_row[:, chunk - 1:chunk]
        decay = jnp.exp(g_row[:, chunk - 1:chunk] - m_last)
        w_end = jnp.exp(b_row[:, chunk - 1:chunk] - b_row + i_row - m_last)
        s_ref[0, h] = decay * state + _dot((v_aug.astype(F32) * w_end).astype(BF16), k)
        m_ref[0, h] = jnp.broadcast_to(m_last, (1, HEAD_W))


def _mlstm(q, k, v, gates, gate_a, norm_g, state0, m0, chunk):
    bsz, seq, _ = q.shape
    chunk = min(chunk, seq)
    rows = 1
    tok = lambda width: pl.BlockSpec((rows, chunk, width), lambda b, c: (b, c, 0))
    st_spec = pl.BlockSpec((rows, N_HEADS, ACC_ROWS, HEAD_W), lambda b, c: (b, 0, 0, 0))
    m_spec = pl.BlockSpec((rows, N_HEADS, 1, HEAD_W), lambda b, c: (b, 0, 0, 0))
    return pl.pallas_call(
        functools.partial(_mlstm_kernel, chunk=chunk),
        grid=(bsz // rows, seq // chunk),
        in_specs=[tok(MIX_W), tok(MIX_W), tok(MIX_W), tok(HEAD_W), tok(MIX_W),
                  _const_spec((1, MIX_W)), st_spec, m_spec],
        out_specs=[tok(MIX_W), st_spec, m_spec],
        out_shape=[jax.ShapeDtypeStruct((bsz, seq, MIX_W), BF16),
                   jax.ShapeDtypeStruct(state0.shape, F32),
                   jax.ShapeDtypeStruct(m0.shape, F32)],
        compiler_params=_params(("arbitrary", "arbitrary")),
        name="mlstm",
    )(q, k, v, gates, gate_a, norm_g, state0, m0)


def _split_maps(q):
    lane = lax.broadcasted_iota(jnp.int32, q.shape, 1)
    zero = jnp.zeros_like(q)
    return jnp.concatenate([jnp.where(lane < DK_B, q, zero), jnp.where(lane >= DK_B, q, zero)], axis=0)


def _diff_lambda(lam_refs, lam_init):
    lq1, lk1, lq2, lk2 = (r[...] for r in lam_refs)
    return (jnp.exp(jnp.sum(lq1 * lk1, axis=-1, keepdims=True))
            - jnp.exp(jnp.sum(lq2 * lk2, axis=-1, keepdims=True)) + lam_init)


def _diff_gate(o, lam_init, sg, gate):
    return (_rms_rows(o, sg) * (1.0 - lam_init) * gate.astype(F32)).astype(BF16)


def _diff_exact_kernel(q_ref, k_ref, vt_ref, gate_ref, bias_ref, lq1_ref, lk1_ref, lq2_ref, lk2_ref,
                       sg_ref, o_ref, m_sc, acc_sc, *, tq, heads, lam_init):
    i = pl.program_id(2)
    tiles_per_map = tq // HEAD_W
    m_sc[...] = jnp.full(m_sc.shape, NEG_INF, F32)
    acc_sc[...] = jnp.zeros(acc_sc.shape, F32)
    ones_rows = jnp.ones((ACC_ROWS - HEAD_W, tq), BF16)
    qq = [_split_maps(q_ref[0, :, h * HEAD_W:(h + 1) * HEAD_W]) for h in range(heads)]

    def block(j, table):
        start = pl.multiple_of(j * tq, tq)
        scores = [_dot_nt(k_ref[0, pl.ds(start, tq), h * HEAD_W:(h + 1) * HEAD_W], qq[h]) for h in range(heads)]
        for h in range(heads):
            hs = slice(h * HEAD_W, (h + 1) * HEAD_W)
            offset = bias_ref[h, 0, 1:2, 0:1] * ((j - i) * tq).astype(F32)
            v_aug = jnp.concatenate([vt_ref[0, hs, pl.ds(start, tq)], ones_rows], axis=0)
            m_old = m_sc[h]
            m_new, p = [], []
            for t in range(2 * tiles_per_map):
                cs = slice(t * HEAD_W, (t + 1) * HEAD_W)
                part = (t % tiles_per_map) * HEAD_W
                s = scores[h][:, cs] + bias_ref[h, table, :, part:part + HEAD_W]
                m_t = jnp.maximum(m_old[:, cs], jnp.max(s, axis=0, keepdims=True) + offset)
                p.append(jnp.exp2(s - (m_t - offset)).astype(BF16))
                m_new.append(m_t)
            m_new = jnp.concatenate(m_new, axis=-1)
            acc_sc[h] = jnp.exp2(m_old - m_new) * acc_sc[h] + _dot(v_aug, jnp.concatenate(p, axis=-1))
            m_sc[h] = m_new

    def earlier(j, carry):
        block(j, 0)
        return carry

    lax.fori_loop(0, i, earlier, 0)
    block(i, 1)
    _diff_finish(acc_sc, tq, heads, (lq1_ref, lk1_ref, lq2_ref, lk2_ref), lam_init, sg_ref, gate_ref, o_ref)


def _diff_fixed_kernel(q_ref, qpos_ref, k_ref, kpos_ref, vt_ref, gate_ref, corr_ref, lq1_ref, lk1_ref, lq2_ref,
                       lk2_ref, sg_ref, o_ref, acc_sc, *, tq, heads, lam_init):
    i = pl.program_id(2)
    acc_sc[...] = jnp.zeros(acc_sc.shape, F32)
    ones_rows = jnp.ones((ACC_ROWS - HEAD_W, tq), BF16)
    qq = [jnp.concatenate([_split_maps(q_ref[0, :, h * HEAD_W:(h + 1) * HEAD_W]),
                           jnp.concatenate([qpos_ref[h], qpos_ref[h]], axis=0)], axis=-1)
          for h in range(heads)]

    def block(j, diagonal):
        keys = pl.ds(pl.multiple_of(j * tq, tq), tq)
        for h in range(heads):
            hs = slice(h * HEAD_W, (h + 1) * HEAD_W)
            k_aug = jnp.concatenate([k_ref[0, keys, hs], kpos_ref[h, keys, :]], axis=-1)
            s = _dot_nt(k_aug, qq[h])
            if diagonal:
                s = s + jnp.concatenate([corr_ref[h], corr_ref[h]], axis=-1)
            v_aug = jnp.concatenate([vt_ref[0, hs, keys], ones_rows], axis=0)
            acc_sc[h] = acc_sc[h] + _dot(v_aug, jnp.exp2(s).astype(BF16))

    def earlier(j, carry):
        block(j, False)
        return carry

    lax.fori_loop(0, i, earlier, 0)
    block(i, True)
    _diff_finish(acc_sc, tq, heads, (lq1_ref, lk1_ref, lq2_ref, lk2_ref), lam_init, sg_ref, gate_ref, o_ref)


def _diff_finish(acc_sc, tq, heads, lam_refs, lam_init, sg_ref, gate_ref, o_ref):
    lam = _diff_lambda(lam_refs, lam_init)
    for h in range(heads):
        hs = slice(h * HEAD_W, (h + 1) * HEAD_W)
        acc = acc_sc[h]
        o_t = (acc[:HEAD_W, :tq] / acc[HEAD_W:HEAD_W + 1, :tq]
               - lam * (acc[:HEAD_W, tq:] / acc[HEAD_W:HEAD_W + 1, tq:]))
        o_ref[0, :, hs] = _diff_gate(o_t.T, lam_init, sg_ref[...], gate_ref[0, :, hs])


def _alibi_tables(tq):
    slopes = jnp.array(ALIBI_SLOPES, F32) * LOG2E
    kk = jnp.arange(tq, dtype=jnp.int32)[:, None]
    qi = jnp.arange(tq, dtype=jnp.int32)[None, :]
    base = slopes[:, None, None] * jnp.broadcast_to(kk.astype(F32), (tq, tq))[None]
    diag = slopes[:, None, None] * (qi - jnp.abs(qi - kk)).astype(F32)[None]
    diag = jnp.where((kk // CHUNK <= qi // CHUNK)[None], diag, NEG_INF)
    return jnp.stack([base, diag], axis=1)


def _truncate_to_bf16_grid(x):
    bits = lax.bitcast_convert_type(x, jnp.uint32) & jnp.uint32(0xFFFF0000)
    return lax.bitcast_convert_type(bits, F32)


def _split3(x):
    hi = _truncate_to_bf16_grid(x)
    mid = _truncate_to_bf16_grid(x - hi)
    return hi.astype(BF16), mid.astype(BF16), (x - hi - mid).astype(BF16)


def _position_tables(seq, tq, ub):
    slopes = jnp.array(ALIBI_SLOPES, F32)[:, None] * LOG2E
    pos = jnp.arange(seq, dtype=F32)[None, :]
    ones = jnp.ones((N_HEADS, seq), BF16)
    lanes = lambda cols: jnp.pad(jnp.stack(cols, axis=-1), ((0, 0), (0, 0), (0, HEAD_W - len(cols))))
    kpos = lanes([*_split3(slopes * pos), ones, ones, ones])
    qpos = lanes([ones, ones, ones, *_split3(-slopes * pos - ub)])
    kk = jnp.arange(tq, dtype=jnp.int32)[:, None]
    qi = jnp.arange(tq, dtype=jnp.int32)[None, :]
    corr = slopes[:, :, None] * (-2.0 * jnp.maximum(kk - qi, 0).astype(F32))[None]
    corr = jnp.where((kk // CHUNK <= qi // CHUNK)[None], corr, NEG_INF)
    return kpos, qpos, corr


def _diff_prompt(q, k, vt, gate, lam_vecs, subln_g, lam_init, tq, heads, ub):
    bsz, seq, _ = q.shape
    tq = min(tq, seq)
    width = heads * HEAD_W
    grid = (bsz, N_HEADS // heads, seq // tq)
    q_spec = pl.BlockSpec((1, tq, width), lambda b, h, i: (b, i, h))
    once = dict(pipeline_mode=pl.Buffered(1))
    k_spec = pl.BlockSpec((1, seq, width), lambda b, h, i: (b, 0, h), **once)
    vt_spec = pl.BlockSpec((1, width, seq), lambda b, h, i: (b, h, 0), **once)
    tail_specs = [_const_spec((1, DK_B))] * 4 + [_const_spec((1, HEAD_W))]
    out_shape = jax.ShapeDtypeStruct((bsz, seq, MIX_W), BF16)
    acc_buf = pltpu.VMEM((heads, ACC_ROWS, 2 * tq), F32)
    sem = _params(("arbitrary", "arbitrary", "arbitrary"))

    def fixed():
        kpos, qpos, corr = _position_tables(seq, tq, ub)
        return pl.pallas_call(
            functools.partial(_diff_fixed_kernel, tq=tq, heads=heads, lam_init=lam_init),
            grid=grid,
            in_specs=[q_spec,
                      pl.BlockSpec((heads, tq, HEAD_W), lambda b, h, i: (h, i, 0)),
                      k_spec,
                      pl.BlockSpec((heads, seq, HEAD_W), lambda b, h, i: (h, 0, 0), **once),
                      vt_spec, q_spec,
                      pl.BlockSpec((heads, tq, tq), lambda b, h, i: (h, 0, 0), **once)] + tail_specs,
            out_specs=q_spec, out_shape=out_shape, scratch_shapes=[acc_buf],
            compiler_params=sem, name="diff_fixed",
        )(q, qpos, k, kpos, vt, gate, corr, *lam_vecs, subln_g)

    def exact():
        return pl.pallas_call(
            functools.partial(_diff_exact_kernel, tq=tq, heads=heads, lam_init=lam_init),
            grid=grid,
            in_specs=[q_spec, k_spec, vt_spec, q_spec,
                      pl.BlockSpec((heads, 2, tq, tq), lambda b, h, i: (h, 0, 0, 0), **once)] + tail_specs,
            out_specs=q_spec, out_shape=out_shape,
            scratch_shapes=[pltpu.VMEM((heads, 1, 2 * tq), F32), acc_buf],
            compiler_params=sem, name="diff_exact",
        )(q, k, vt, gate, _alibi_tables(tq), *lam_vecs, subln_g)

    return lax.cond(ub <= FIXED_SHIFT_MAX, fixed, exact)


def _diff_sample_kernel(q_ref, k_ref, v_ref, kp_ref, vp_ref, gate_ref, slope_ref, lq1_ref, lk1_ref, lq2_ref,
                        lk2_ref, sg_ref, o_ref, *, lam_init):
    rows = q_ref.shape[1]
    past = kp_ref.shape[1] // N_HEADS
    lam = _diff_lambda((lq1_ref, lk1_ref, lq2_ref, lk2_ref), lam_init)
    r = lax.broadcasted_iota(jnp.int32, (2 * rows, 1), 0)
    r = jnp.where(r >= rows, r - rows, r)
    dist_past = (past + r - lax.broadcasted_iota(jnp.int32, (2 * rows, past), 1)).astype(F32)
    dist_new = jnp.abs(r - lax.broadcasted_iota(jnp.int32, (2 * rows, rows), 1)).astype(F32)
    for h in range(N_HEADS):
        hs = slice(h * HEAD_W, (h + 1) * HEAD_W)
        slope = slope_ref[h][:, 0:1]
        qq = _split_maps(q_ref[0, :, hs])
        k_past = kp_ref[0, pl.ds(h, past, stride=N_HEADS), :].astype(BF16)
        v_past = vp_ref[0, pl.ds(h, past, stride=N_HEADS), :].astype(BF16)
        s_past = _dot_nt(qq, k_past) - slope * dist_past
        s_new = _dot_nt(qq, k_ref[0, :, hs]) - slope * dist_new
        m = jnp.maximum(jnp.max(s_past, axis=-1, keepdims=True), jnp.max(s_new, axis=-1, keepdims=True))
        acc = (_dot(jnp.exp2(s_past - m).astype(BF16), jnp.concatenate([v_past, _ones_col(past)], axis=-1))
               + _dot(jnp.exp2(s_new - m).astype(BF16), jnp.concatenate([v_ref[0, :, hs], _ones_col(rows)], axis=-1)))
        o = (acc[:rows, :HEAD_W] / acc[:rows, HEAD_W:HEAD_W + 1]
             - lam * (acc[rows:, :HEAD_W] / acc[rows:, HEAD_W:HEAD_W + 1]))
        o_ref[0, :, hs] = _diff_gate(o, lam_init, sg_ref[...], gate_ref[0, :, hs])


def _diff_sample(q, k_new, v_new, k_past, v_past, gate, slopes, lam_vecs, subln_g, lam_init):
    bsz, rows, _ = q.shape
    past = k_past.shape[1]
    tok = pl.BlockSpec((1, rows, MIX_W), lambda b: (b, 0, 0))
    cache = pl.BlockSpec((1, past * N_HEADS, HEAD_W), lambda b: (b, 0, 0))
    head_rows = lambda a: a.reshape(bsz, past * N_HEADS, HEAD_W)
    return pl.pallas_call(
        functools.partial(_diff_sample_kernel, lam_init=lam_init),
        grid=(bsz,),
        in_specs=[tok, tok, tok, cache, cache, tok, _const_spec((N_HEADS, 1, HEAD_W))]
                 + [_const_spec((1, DK_B))] * 4 + [_const_spec((1, HEAD_W))],
        out_specs=tok,
        out_shape=jax.ShapeDtypeStruct((bsz, rows, MIX_W), BF16),
        compiler_params=_params(("arbitrary",)),
        name="diff_sample",
    )(q, k_new, v_new, head_rows(k_past), head_rows(v_past), gate, slopes, *lam_vecs, subln_g)


def _cross_attn(xq, gate, mem_k, mem_v):
    outs = []
    for h in range(N_HEADS):
        sl = slice(h * HEAD_W, (h + 1) * HEAD_W)
        s = _dot_nt(xq[:, sl], mem_k[:, sl])
        p = jnp.exp2(s - jnp.max(s, axis=-1, keepdims=True))
        outs.append(_dot(p.astype(BF16), mem_v[:, sl]) / jnp.sum(p, axis=-1, keepdims=True))
    return (jnp.concatenate(outs, axis=-1) * gate).astype(BF16)


def _out_even_kernel(x_ref, oa_ref, ob_ref, xq_ref, gx_ref, mk_ref, mv_ref, w_ref, y_ref):
    ox = _cross_attn(xq_ref[0], gx_ref[0].astype(F32), mk_ref[0], mv_ref[0])
    mixed = jnp.concatenate([oa_ref[0], ob_ref[0], ox], axis=-1)
    y_ref[0] = x_ref[0] + _dot(mixed, w_ref[...])


def _out_even(x, oa, ob, xq, gx, mem_k, mem_v, w_out, tm):
    bsz, seq, _ = x.shape
    tm = min(tm, seq)
    tok = lambda width: pl.BlockSpec((1, tm, width), lambda b, i: (b, i, 0))
    mem_spec = pl.BlockSpec((1, N_MEM, MIX_W), lambda b, i: (b, 0, 0))
    return pl.pallas_call(
        _out_even_kernel,
        grid=(bsz, seq // tm),
        in_specs=[tok(D_MODEL), tok(MIX_W), tok(MIX_W), tok(MIX_W), tok(MIX_W), mem_spec, mem_spec,
                  _const_spec((3 * MIX_W, D_MODEL))],
        out_specs=tok(D_MODEL),
        out_shape=jax.ShapeDtypeStruct(x.shape, F32),
        compiler_params=_params(("arbitrary", "arbitrary")),
        name="out_even",
    )(x, oa, ob, xq, gx, mem_k, mem_v, w_out)


_O_CU, _O_CG, _O_CZ = 0, D_MODEL, 2 * D_MODEL
_O_XQ, _O_XZ = 3 * D_MODEL, 3 * D_MODEL + MIX_W


def _odd_kernel(x_ref, hist_ref, g_ref, w_in_ref, cw_ref, cb_ref, lg_ref, lb_ref, xqg_ref, mk_ref, mv_ref,
                w_out_ref, y_ref, tail_ref, u_sc, conv_sc, *, tm):
    i = pl.program_id(1)

    @pl.when(i == 0)
    def _():
        u_sc[0:CONV_HIST, :] = hist_ref[0]

    xn = _rms_rows(x_ref[0], g_ref[...]).astype(BF16)

    def proj(c0, width):
        return _dot(xn, w_in_ref[:, c0:c0 + width])

    u = proj(_O_CU, D_MODEL) * _sigmoid(proj(_O_CG, D_MODEL))
    u_sc[CONV_HIST:CONV_HIST + tm, :] = u
    rows = min(CONV_ROWS, tm)
    for c0 in range(0, D_MODEL, HEAD_W):
        cs = slice(c0, c0 + HEAD_W)
        for r0 in range(0, tm, rows):
            acc = jnp.broadcast_to(cb_ref[:, cs], (rows, HEAD_W))
            span = rows + CONV_HIST
            aligned = u_sc[r0:r0 + span, cs]
            for phase in range(SUBLANES):
                offs = [o for o in range(CONV_HIST - (CONV_W - 1), CONV_HIST + 1) if o % SUBLANES == phase]
                win = pltpu.roll(aligned, span - phase, axis=0) if phase else aligned
                for off in offs:
                    j = off - (CONV_HIST - (CONV_W - 1))
                    acc = acc + cw_ref[j:j + 1, cs] * win[off - phase:off - phase + rows]
            conv_sc[r0:r0 + rows, cs] = acc
    acc = conv_sc[...]
    mu = jnp.mean(acc, axis=-1, keepdims=True)
    cen = acc - mu
    var = jnp.mean(cen * cen, axis=-1, keepdims=True)
    c = _silu(cen * lax.rsqrt(var + EPS) * lg_ref[...] + lb_ref[...])
    c = (c * _silu(proj(_O_CZ, D_MODEL))).astype(BF16)
    xq = proj(_O_XQ, MIX_W)
    xqg = xqg_ref[...] * ((HEAD_W ** -0.5) * LOG2E)
    xq = jnp.concatenate([_rms_rows(xq[:, h * HEAD_W:(h + 1) * HEAD_W], xqg) for h in range(N_HEADS)], axis=-1)
    ox = _cross_attn(xq.astype(BF16), _silu(proj(_O_XZ, MIX_W)), mk_ref[0], mv_ref[0])
    y_ref[0] = x_ref[0] + _dot(c, w_out_ref[0:D_MODEL, :]) + _dot(ox, w_out_ref[D_MODEL:D_MODEL + MIX_W, :])
    tail = u_sc[tm:tm + CONV_HIST, :]
    u_sc[0:CONV_HIST, :] = tail
    tail_ref[0] = tail


def _odd_layer(x, hist, norm_g, w_in, conv_w, conv_b, ln_g, ln_b, xqg, mem_k, mem_v, w_out, tm):
    bsz, seq, _ = x.shape
    tm = min(tm, seq)
    tok = pl.BlockSpec((1, tm, D_MODEL), lambda b, i: (b, i, 0))
    hist_spec = pl.BlockSpec((1, CONV_HIST, D_MODEL), lambda b, i: (b, 0, 0))
    mem_spec = pl.BlockSpec((1, N_MEM, MIX_W), lambda b, i: (b, 0, 0))
    vec = _const_spec((1, D_MODEL))
    return pl.pallas_call(
        functools.partial(_odd_kernel, tm=tm),
        grid=(bsz, seq // tm),
        in_specs=[tok, hist_spec, vec, _const_spec(w_in.shape), _const_spec(conv_w.shape), vec, vec, vec,
                  _const_spec((1, HEAD_W)), mem_spec, mem_spec, _const_spec(w_out.shape)],
        out_specs=[tok, hist_spec],
        out_shape=[jax.ShapeDtypeStruct(x.shape, F32), jax.ShapeDtypeStruct((bsz, CONV_HIST, D_MODEL), F32)],
        scratch_shapes=[pltpu.VMEM((CONV_HIST + tm, D_MODEL), F32), pltpu.VMEM((tm, D_MODEL), F32)],
        compiler_params=_params(("arbitrary", "arbitrary")),
        name="odd_layer",
    )(x, hist, norm_g, w_in, conv_w, conv_b, ln_g, ln_b, xqg, mem_k, mem_v, w_out)


def _even_layer(x, mem_k, mem_v, hist, wts, lam_init, tm_in, tm_out, chunk, tq):
    (norm_g, w_main, w_gate, gate_bias, grp, qn, kn, xqg, mlstm_g, slopes, lam_vecs, subln_g, w_out) = wts
    bsz, seq, _ = x.shape
    (qa, ka, va, ga, gates, qb, kb, kbh, vb, vbh, gb, xq, gx) = _in_even(
        x.reshape(bsz * seq, D_MODEL), norm_g, w_main, w_gate, gate_bias, grp, qn, kn, xqg, tm_in,
        v_transposed_seq=seq if hist is None else None)
    tok = lambda a: a.reshape(bsz, seq, a.shape[-1])
    if hist is None:
        state0 = jnp.zeros((bsz, N_HEADS, ACC_ROWS, HEAD_W), F32)
        m0 = jnp.zeros((bsz, N_HEADS, 1, HEAD_W), F32)
    else:
        k_past, v_past, c0, n0, m0 = hist
        state0 = jnp.concatenate([jnp.swapaxes(c0, -1, -2), n0[..., None, :],
                                  jnp.zeros((bsz, N_HEADS, ACC_ROWS - HEAD_W - 1, HEAD_W), F32)], axis=-2)
        m0 = jnp.broadcast_to(m0[..., None, None], (bsz, N_HEADS, 1, HEAD_W))
    oa, state1, m1 = _mlstm(tok(qa), tok(ka), tok(va), tok(gates), tok(ga), mlstm_g, state0, m0, chunk)
    if hist is None:
        ub = (8.0 * LOG2E * 1.01) * jnp.max(jnp.abs(qn)) * jnp.max(jnp.abs(kn))
        ob = _diff_prompt(tok(qb), tok(kbh), vbh, tok(gb), lam_vecs, subln_g, lam_init, tq, heads=4, ub=ub)
    else:
        ob = _diff_sample(tok(qb), tok(kbh), tok(vbh), k_past, v_past, tok(gb), slopes, lam_vecs, subln_g,
                          lam_init)
    y = _out_even(x, oa, ob, tok(xq), tok(gx), mem_k, mem_v, w_out, tm_out)
    return (y, kb.reshape(bsz, seq, N_HEADS, HEAD_W), vb.reshape(bsz, seq, N_HEADS, HEAD_W),
            jnp.swapaxes(state1[..., :HEAD_W, :], -1, -2), state1[..., HEAD_W, :], m1[:, :, 0, 0])


def kernel(x_prompt, x_sample, mem_prompt, cache_xk, cache_xv, cache_k, cache_v, state_C, state_n, state_m,
           state_conv, norm_g, w_in_a, b_ig, b_fg, mlstm_norm_g, qn_g, kn_g, lam_q1, lam_k1, lam_q2, lam_k2,
           subln_g, w_out_a, w_in_c, conv_w, conv_b, conv_ln_g, conv_ln_b, w_out_c, mem_norm_g, w_mem_kv,
           xq_norm_g, xk_norm_g):
    depth = norm_g.shape[0]
    bsz = x_prompt.shape[0]
    dec = x_sample.shape[0]
    p_xk, p_xv = _mem_kv(mem_prompt, mem_norm_g, w_mem_kv, xk_norm_g)
    mem_k_p, mem_v_p = p_xk.astype(BF16), p_xv.astype(BF16)
    mem_k_s = cache_xk.reshape(depth, dec, N_MEM, MIX_W).astype(BF16)
    mem_v_s = cache_xv.reshape(depth, dec, N_MEM, MIX_W).astype(BF16)

    lane_grp = jnp.arange(MIX_W) // DK_B
    grp = (lane_grp[:, None] == lane_grp[None, :]).astype(BF16)
    slopes = jnp.broadcast_to(jnp.array(ALIBI_SLOPES, F32)[:, None, None] * LOG2E, (N_HEADS, 1, HEAD_W))
    n_gate = 2 * N_HEADS
    gate0 = 5 * MIX_W

    yp, ys = x_prompt, x_sample
    outs = {name: [] for name in ("p_k", "p_v", "p_C", "p_n", "p_m", "p_conv",
                                  "s_k", "s_v", "s_C", "s_n", "s_m", "s_conv")}
    for layer in range(depth):
        if layer % 2 == 0:
            e = layer // 2
            w = w_in_a[e]
            w_main = jnp.concatenate([w[:, :gate0], w[:, gate0 + n_gate:]], axis=1).astype(BF16)
            w_gate = jnp.pad(w[:, gate0:gate0 + n_gate], ((0, 0), (0, HEAD_W - n_gate))).astype(BF16)
            gate_bias = jnp.pad(jnp.concatenate([b_ig[e], b_fg[e]]), (0, HEAD_W - n_gate)).reshape(1, HEAD_W)
            wts = (norm_g[layer].reshape(1, D_MODEL), w_main, w_gate, gate_bias, grp,
                   jnp.tile(qn_g[e], MIX_W // DK_B).reshape(1, MIX_W),
                   jnp.tile(kn_g[e], MIX_W // DK_B).reshape(1, MIX_W),
                   xq_norm_g[layer].reshape(1, HEAD_W), mlstm_norm_g[e].reshape(1, MIX_W), slopes,
                   tuple(v[e].reshape(1, DK_B) for v in (lam_q1, lam_k1, lam_q2, lam_k2)),
                   subln_g[e].reshape(1, HEAD_W), w_out_a[e].astype(BF16))
            lam_init = _lambda_init(layer)
            yp, k_new, v_new, c1, n1, m1 = _even_layer(yp, mem_k_p[layer], mem_v_p[layer], None, wts, lam_init,
                                                       tm_in=256, tm_out=512, chunk=256, tq=512)
            for name, val in zip(("p_k", "p_v", "p_C", "p_n", "p_m"), (k_new, v_new, c1, n1, m1)):
                outs[name].append(val)
            hist = (cache_k[e], cache_v[e], state_C[e], state_n[e], state_m[e])
            ys, k_new, v_new, c1, n1, m1 = _even_layer(ys, mem_k_s[layer], mem_v_s[layer], hist, wts, lam_init,
                                                       tm_in=256, tm_out=512, chunk=256, tq=512)
            for name, val in zip(("s_k", "s_v", "s_C", "s_n", "s_m"), (k_new, v_new, c1, n1, m1)):
                outs[name].append(val)
        else:
            o = layer // 2
            vec = lambda a: a.reshape(1, -1)
            wts = (vec(norm_g[layer]), w_in_c[o].astype(BF16), jnp.pad(conv_w[o], ((0, 1), (0, 0))),
                   vec(conv_b[o]), vec(conv_ln_g[o]), vec(conv_ln_b[o]), vec(xq_norm_g[layer]))
            w_out = w_out_c[o].astype(BF16)
            pad = CONV_HIST - (CONV_W - 1)
            zero_hist = jnp.zeros((bsz, CONV_HIST, D_MODEL), F32)
            yp, tail = _odd_layer(yp, zero_hist, *wts, mem_k_p[layer], mem_v_p[layer], w_out, tm=512)
            outs["p_conv"].append(tail[:, pad:])
            hist = jnp.pad(state_conv[o], ((0, 0), (pad, 0), (0, 0)))
            ys, tail = _odd_layer(ys, hist, *wts, mem_k_s[layer], mem_v_s[layer], w_out, tm=256)
            outs["s_conv"].append(tail[:, pad:])

    head5 = lambda a: a.reshape(a.shape[:-1] + (N_HEADS, HEAD_W))
    st = {name: jnp.stack(vals) for name, vals in outs.items()}
    return (yp, ys, head5(p_xk), head5(p_xv), st["p_k"], st["p_v"], st["p_C"], st["p_n"], st["p_m"], st["p_conv"],
            st["s_k"], st["s_v"], st["s_C"], st["s_n"], st["s_m"], st["s_conv"])
```

```python
import functools
import math

import jax
import jax.numpy as jnp
from jax import lax
from jax.experimental import pallas as pl
from jax.experimental.pallas import tpu as pltpu

F32 = jnp.float32
BF16 = jnp.bfloat16

D_MODEL = 1024
CHUNK = 64
EPS = 1e-6
N_HEADS = 4
HEAD_W = 128
MIX_W = N_HEADS * HEAD_W
DK_B = 64
CONV_W = 31
CONV_HIST = 32
CONV_ROWS = 128
SUBLANES = 8
N_MEM = 256
ALIBI_SLOPES = tuple(2.0 ** (-8.0 * (h + 1) / N_HEADS) for h in range(N_HEADS))
VMEM_LIMIT = 56 * 1024 * 1024

NEG_INF = float("-inf")
LOG2E = math.log2(math.e)
FIXED_SHIFT_MAX = 32.0
ACC_ROWS = HEAD_W + 16


def _lambda_init(layer):
    return 0.8 - 0.6 * math.exp(-0.3 * layer)


def _params(sem):
    return pltpu.CompilerParams(dimension_semantics=sem, vmem_limit_bytes=VMEM_LIMIT)


def _const_spec(shape):
    zeros = (0,) * len(shape)
    return pl.BlockSpec(shape, lambda *_: zeros)


def _rms_rows(x, g):
    return x * lax.rsqrt(jnp.mean(x * x, axis=-1, keepdims=True) + EPS) * g


def _sigmoid(x):
    return 1.0 / (1.0 + jnp.exp2(x * (-LOG2E)))


def _silu(x):
    return x * _sigmoid(x)


def _dot(a, b):
    return jnp.dot(a, b, preferred_element_type=F32)


def _dot_nt(a, b):
    return lax.dot_general(a, b, (((1,), (1,)), ((), ())), preferred_element_type=F32)


def _ones_col(rows):
    lane = lax.broadcasted_iota(jnp.int32, (rows, HEAD_W), 1)
    return jnp.where(lane == 0, 1.0, 0.0).astype(BF16)


def _mem_kv_kernel(mem_ref, g_ref, w_ref, kg_ref, k_ref, v_ref):
    xn = _rms_rows(mem_ref[0], g_ref[0]).astype(BF16)
    kv = _dot(xn, w_ref[0])
    kg = kg_ref[0]
    for h in range(N_HEADS):
        sl = slice(h * HEAD_W, (h + 1) * HEAD_W)
        k_ref[0, 0, :, sl] = _rms_rows(kv[:, sl], kg)
    v_ref[0, 0] = kv[:, MIX_W:]


def _mem_kv(mem, g, w_kv, kg):
    depth = w_kv.shape[0]
    bsz, n_mem, _ = mem.shape
    out = jax.ShapeDtypeStruct((depth, bsz, n_mem, MIX_W), F32)
    return pl.pallas_call(
        _mem_kv_kernel,
        grid=(depth, bsz),
        in_specs=[
            pl.BlockSpec((1, n_mem, D_MODEL), lambda l, b: (b, 0, 0)),
            pl.BlockSpec((1, 1, D_MODEL), lambda l, b: (l, 0, 0)),
            pl.BlockSpec((1, D_MODEL, 2 * MIX_W), lambda l, b: (l, 0, 0)),
            pl.BlockSpec((1, 1, HEAD_W), lambda l, b: (l, 0, 0)),
        ],
        out_specs=[pl.BlockSpec((1, 1, n_mem, MIX_W), lambda l, b: (l, b, 0, 0))] * 2,
        out_shape=[out, out],
        compiler_params=_params(("arbitrary", "arbitrary")),
        name="mem_kv",
    )(mem, g.reshape(depth, 1, D_MODEL), w_kv.astype(BF16), kg.reshape(depth, 1, HEAD_W))


_E_AQ, _E_AK, _E_AV, _E_AO, _E_AZ, _E_BQ, _E_BK, _E_BV, _E_BZ, _E_XQ, _E_XZ = (i * MIX_W for i in range(11))
_E_MAIN = 11 * MIX_W


def _group_mean_sq(y, grp):
    sq = y * y
    hi = sq.astype(BF16)
    lo = (sq - hi.astype(F32)).astype(BF16)
    return (_dot(hi, grp) + _dot(lo, grp)) * (1.0 / DK_B)


def _store_head_rows(ref, x):
    tokens = x.shape[0]
    for h in range(N_HEADS):
        ref[pl.ds(h, tokens, stride=N_HEADS), :] = x[:, h * HEAD_W:(h + 1) * HEAD_W]


def _in_even_kernel(x_ref, g_ref, w_ref, wg_ref, gbias_ref, grp_ref, qn_ref, kn_ref, xqg_ref,
                    qa_ref, ka_ref, va_ref, ga_ref, gt_ref, qb_ref, kb_ref, kbh_ref, vb_ref, vbh_ref,
                    gb_ref, xq_ref, gx_ref):
    xn = _rms_rows(x_ref[...], g_ref[...]).astype(BF16)

    def proj(c0):
        return _dot(xn, w_ref[:, c0:c0 + MIX_W])

    qa_ref[...] = proj(_E_AQ).astype(BF16)
    ka_ref[...] = (proj(_E_AK) * (HEAD_W ** -0.5)).astype(BF16)
    va_ref[...] = proj(_E_AV).astype(BF16)
    ga_ref[...] = (_sigmoid(proj(_E_AO)) * _silu(proj(_E_AZ))).astype(BF16)
    gt = _dot(xn, wg_ref[...]) + gbias_ref[...]
    lane = lax.broadcasted_iota(jnp.int32, gt.shape, 1)
    log_sig = jnp.minimum(gt, 0.0) - jnp.log1p(jnp.exp(-jnp.abs(gt)))
    gt_ref[...] = jnp.where((lane >= N_HEADS) & (lane < 2 * N_HEADS), log_sig, gt)
    grp = grp_ref[...]
    bq = proj(_E_BQ)
    q_scale = (DK_B ** -0.5) * LOG2E
    qb_ref[...] = (bq * lax.rsqrt(_group_mean_sq(bq, grp) + EPS) * qn_ref[...] * q_scale).astype(BF16)
    bk = proj(_E_BK)
    kb = bk * lax.rsqrt(_group_mean_sq(bk, grp) + EPS) * kn_ref[...]
    _store_head_rows(kb_ref, kb)
    kbh_ref[...] = kb.astype(BF16)
    bv = proj(_E_BV)
    _store_head_rows(vb_ref, bv)
    if len(vbh_ref.shape) == 3:
        vbh_ref[0] = bv.T.astype(BF16)
    else:
        vbh_ref[...] = bv.astype(BF16)
    gb_ref[...] = _silu(proj(_E_BZ)).astype(BF16)
    xq = proj(_E_XQ)
    xqg = xqg_ref[...] * ((HEAD_W ** -0.5) * LOG2E)
    for h in range(N_HEADS):
        sl = slice(h * HEAD_W, (h + 1) * HEAD_W)
        xq_ref[:, sl] = _rms_rows(xq[:, sl], xqg).astype(BF16)
    gx_ref[...] = _silu(proj(_E_XZ)).astype(BF16)


def _in_even(x2d, norm_g, w_main, w_gate, gate_bias, grp, qn, kn, xqg, tm, v_transposed_seq):
    m = x2d.shape[0]
    tm = min(tm, m)
    row = lambda width: pl.BlockSpec((tm, width), lambda i: (i, 0))
    half = jax.ShapeDtypeStruct((m, MIX_W), BF16)
    full = jax.ShapeDtypeStruct((m * N_HEADS, HEAD_W), F32)
    head_rows = pl.BlockSpec((tm * N_HEADS, HEAD_W), lambda i: (i, 0))
    out_shape = [half, half, half, half, jax.ShapeDtypeStruct((m, HEAD_W), F32),
                 half, full, half, full, half, half, half, half]
    out_specs = [row(MIX_W)] * 4 + [row(HEAD_W)] + [row(MIX_W)] * 8
    out_specs[6] = out_specs[8] = head_rows
    if v_transposed_seq is not None:
        tiles = v_transposed_seq // tm
        out_shape[9] = jax.ShapeDtypeStruct((m // v_transposed_seq, MIX_W, v_transposed_seq), BF16)
        out_specs[9] = pl.BlockSpec((1, MIX_W, tm), lambda i: (i // tiles, 0, i % tiles))
    return pl.pallas_call(
        _in_even_kernel,
        grid=(m // tm,),
        in_specs=[
            row(D_MODEL),
            _const_spec((1, D_MODEL)),
            _const_spec((D_MODEL, _E_MAIN)),
            _const_spec((D_MODEL, HEAD_W)),
            _const_spec((1, HEAD_W)),
            _const_spec((MIX_W, MIX_W)),
            _const_spec((1, MIX_W)),
            _const_spec((1, MIX_W)),
            _const_spec((1, HEAD_W)),
        ],
        out_specs=out_specs,
        out_shape=out_shape,
        compiler_params=_params(("arbitrary",)),
        name="in_even",
    )(x2d, norm_g, w_main, w_gate, gate_bias, grp, qn, kn, xqg)


def _mlstm_kernel(q_ref, k_ref, v_ref, gt_ref, ga_ref, ng_ref, s0_ref, m0_ref, o_ref, s_ref, m_ref, *, chunk):
    c = pl.program_id(1)

    @pl.when(c == 0)
    def _():
        s_ref[...] = s0_ref[...]
        m_ref[...] = m0_ref[...]

    src = lax.broadcasted_iota(jnp.int32, (chunk, chunk), 0)
    tgt = lax.broadcasted_iota(jnp.int32, (chunk, chunk), 1)
    causal = src <= tgt
    gt = gt_ref[0]
    gt_t = gt.T
    cum = jnp.dot((tgt <= src).astype(F32), gt, preferred_element_type=F32, precision=lax.Precision.HIGHEST)
    cum_t = jnp.dot(gt_t, causal.astype(F32), preferred_element_type=F32, precision=lax.Precision.HIGHEST)
    ones_rows = jnp.ones((ACC_ROWS - HEAD_W, chunk), BF16)
    for h in range(N_HEADS):
        sl = slice(h * HEAD_W, (h + 1) * HEAD_W)
        b_row = cum_t[N_HEADS + h:N_HEADS + h + 1, :]
        i_row = gt_t[h:h + 1, :]
        c_col = gt[:, h:h + 1] - cum[:, N_HEADS + h:N_HEADS + h + 1]
        state = s_ref[0, h]
        g_row = b_row + m_ref[0, h][:, 0:1]
        logw = jnp.where(causal, c_col + b_row, NEG_INF)
        m_row = jnp.maximum(g_row, jnp.max(logw, axis=0, keepdims=True))
        w_intra = jnp.exp(logw - m_row)
        w_inter = jnp.exp(g_row - m_row)
        q = q_ref[0, :, sl]
        k = k_ref[0, :, sl]
        v_aug = jnp.concatenate([v_ref[0, :, sl].astype(F32).T.astype(BF16), ones_rows], axis=0)
        sc = (_dot_nt(k, q) * w_intra).astype(BF16)
        inter = _dot_nt(state.astype(BF16), q)
        intra = _dot(v_aug, sc)
        num = w_inter * inter[:HEAD_W] + intra[:HEAD_W]
        den = w_inter * inter[HEAD_W:HEAD_W + 1] + intra[HEAD_W:HEAD_W + 1]
        hid = num / jnp.maximum(jnp.abs(den), jnp.exp(-m_row))
        hid = hid * lax.rsqrt(jnp.mean(hid * hid, axis=0, keepdims=True) + EPS)
        o_ref[0, :, sl] = (hid.T * ng_ref[:, sl] * ga_ref[0, :, sl].astype(F32)).astype(BF16)
        m_last = m_row[:, chunk - 1:chunk]
        decay = jnp.exp(g_row[:, chunk - 1:chunk] - m_last)
        w_end = jnp.exp(b_row[:, chunk - 1:chunk] - b_row + i_row - m_last)
        s_ref[0, h] = decay * state + _dot((v_aug.astype(F32) * w_end).astype(BF16), k)
        m_ref[0, h] = jnp.broadcast_to(m_last, (1, HEAD_W))


def _mlstm(q, k, v, gates, gate_a, norm_g, state0, m0, chunk):
    bsz, seq, _ = q.shape
    chunk = min(chunk, seq)
    rows = 1
    tok = lambda width: pl.BlockSpec((rows, chunk, width), lambda b, c: (b, c, 0))
    st_spec = pl.BlockSpec((rows, N_HEADS, ACC_ROWS, HEAD_W), lambda b, c: (b, 0, 0, 0))
    m_spec = pl.BlockSpec((rows, N_HEADS, 1, HEAD_W), lambda b, c: (b, 0, 0, 0))
    return pl.pallas_call(
        functools.partial(_mlstm_kernel, chunk=chunk),
        grid=(bsz // rows, seq // chunk),
        in_specs=[tok(MIX_W), tok(MIX_W), tok(MIX_W), tok(HEAD_W), tok(MIX_W),
                  _const_spec((1, MIX_W)), st_spec, m_spec],
        out_specs=[tok(MIX_W), st_spec, m_spec],
        out_shape=[jax.ShapeDtypeStruct((bsz, seq, MIX_W), BF16),
                   jax.ShapeDtypeStruct(state0.shape, F32),
                   jax.ShapeDtypeStruct(m0.shape, F32)],
        compiler_params=_params(("arbitrary", "arbitrary")),
        name="mlstm",
    )(q, k, v, gates, gate_a, norm_g, state0, m0)


def _split_maps(q):
    lane = lax.broadcasted_iota(jnp.int32, q.shape, 1)
    zero = jnp.zeros_like(q)
    return jnp.concatenate([jnp.where(lane < DK_B, q, zero), jnp.where(lane >= DK_B, q, zero)], axis=0)


def _diff_lambda(lam_refs, lam_init):
    lq1, lk1, lq2, lk2 = (r[...] for r in lam_refs)
    return (jnp.exp(jnp.sum(lq1 * lk1, axis=-1, keepdims=True))
            - jnp.exp(jnp.sum(lq2 * lk2, axis=-1, keepdims=True)) + lam_init)


def _diff_gate(o, lam_init, sg, gate):
    return (_rms_rows(o, sg) * (1.0 - lam_init) * gate.astype(F32)).astype(BF16)


def _diff_exact_kernel(q_ref, k_ref, vt_ref, gate_ref, bias_ref, lq1_ref, lk1_ref, lq2_ref, lk2_ref,
                       sg_ref, o_ref, m_sc, acc_sc, *, tq, heads, lam_init):
    i = pl.program_id(2)
    tiles_per_map = tq // HEAD_W
    m_sc[...] = jnp.full(m_sc.shape, NEG_INF, F32)
    acc_sc[...] = jnp.zeros(acc_sc.shape, F32)
    ones_rows = jnp.ones((ACC_ROWS - HEAD_W, tq), BF16)
    qq = [_split_maps(q_ref[0, :, h * HEAD_W:(h + 1) * HEAD_W]) for h in range(heads)]

    def block(j, table):
        start = pl.multiple_of(j * tq, tq)
        scores = [_dot_nt(k_ref[0, pl.ds(start, tq), h * HEAD_W:(h + 1) * HEAD_W], qq[h]) for h in range(heads)]
        for h in range(heads):
            hs = slice(h * HEAD_W, (h + 1) * HEAD_W)
            offset = bias_ref[h, 0, 1:2, 0:1] * ((j - i) * tq).astype(F32)
            v_aug = jnp.concatenate([vt_ref[0, hs, pl.ds(start, tq)], ones_rows], axis=0)
            m_old = m_sc[h]
            m_new, p = [], []
            for t in range(2 * tiles_per_map):
                cs = slice(t * HEAD_W, (t + 1) * HEAD_W)
                part = (t % tiles_per_map) * HEAD_W
                s = scores[h][:, cs] + bias_ref[h, table, :, part:part + HEAD_W]
                m_t = jnp.maximum(m_old[:, cs], jnp.max(s, axis=0, keepdims=True) + offset)
                p.append(jnp.exp2(s - (m_t - offset)).astype(BF16))
                m_new.append(m_t)
            m_new = jnp.concatenate(m_new, axis=-1)
            acc_sc[h] = jnp.exp2(m_old - m_new) * acc_sc[h] + _dot(v_aug, jnp.concatenate(p, axis=-1))
            m_sc[h] = m_new

    def earlier(j, carry):
        block(j, 0)
        return carry

    lax.fori_loop(0, i, earlier, 0)
    block(i, 1)
    _diff_finish(acc_sc, tq, heads, (lq1_ref, lk1_ref, lq2_ref, lk2_ref), lam_init, sg_ref, gate_ref, o_ref)


def _diff_fixed_kernel(q_ref, qpos_ref, k_ref, kpos_ref, vt_ref, gate_ref, corr_ref, lq1_ref, lk1_ref, lq2_ref,
                       lk2_ref, sg_ref, o_ref, acc_sc, *, tq, heads, lam_init):
    i = pl.program_id(2)
    acc_sc[...] = jnp.zeros(acc_sc.shape, F32)
    ones_rows = jnp.ones((ACC_ROWS - HEAD_W, tq), BF16)
    qq = [jnp.concatenate([_split_maps(q_ref[0, :, h * HEAD_W:(h + 1) * HEAD_W]),
                           jnp.concatenate([qpos_ref[h], qpos_ref[h]], axis=0)], axis=-1)
          for h in range(heads)]

    def block(j, diagonal):
        keys = pl.ds(pl.multiple_of(j * tq, tq), tq)
        for h in range(heads):
            hs = slice(h * HEAD_W, (h + 1) * HEAD_W)
            k_aug = jnp.concatenate([k_ref[0, keys, hs], kpos_ref[h, keys, :]], axis=-1)
            s = _dot_nt(k_aug, qq[h])
            if diagonal:
                s = s + jnp.concatenate([corr_ref[h], corr_ref[h]], axis=-1)
            v_aug = jnp.concatenate([vt_ref[0, hs, keys], ones_rows], axis=0)
            acc_sc[h] = acc_sc[h] + _dot(v_aug, jnp.exp2(s).astype(BF16))

    def earlier(j, carry):
        block(j, False)
        return carry

    lax.fori_loop(0, i, earlier, 0)
    block(i, True)
    _diff_finish(acc_sc, tq, heads, (lq1_ref, lk1_ref, lq2_ref, lk2_ref), lam_init, sg_ref, gate_ref, o_ref)


def _diff_finish(acc_sc, tq, heads, lam_refs, lam_init, sg_ref, gate_ref, o_ref):
    lam = _diff_lambda(lam_refs, lam_init)
    for h in range(heads):
        hs = slice(h * HEAD_W, (h + 1) * HEAD_W)
        acc = acc_sc[h]
        o_t = (acc[:HEAD_W, :tq] / acc[HEAD_W:HEAD_W + 1, :tq]
               - lam * (acc[:HEAD_W, tq:] / acc[HEAD_W:HEAD_W + 1, tq:]))
        o_ref[0, :, hs] = _diff_gate(o_t.T, lam_init, sg_ref[...], gate_ref[0, :, hs])


def _alibi_tables(tq):
    slopes = jnp.array(ALIBI_SLOPES, F32) * LOG2E
    kk = jnp.arange(tq, dtype=jnp.int32)[:, None]
    qi = jnp.arange(tq, dtype=jnp.int32)[None, :]
    base = slopes[:, None, None] * jnp.broadcast_to(kk.astype(F32), (tq, tq))[None]
    diag = slopes[:, None, None] * (qi - jnp.abs(qi - kk)).astype(F32)[None]
    diag = jnp.where((kk // CHUNK <= qi // CHUNK)[None], diag, NEG_INF)
    return jnp.stack([base, diag], axis=1)


def _truncate_to_bf16_grid(x):
    bits = lax.bitcast_convert_type(x, jnp.uint32) & jnp.uint32(0xFFFF0000)
    return lax.bitcast_convert_type(bits, F32)


def _split3(x):
    hi = _truncate_to_bf16_grid(x)
    mid = _truncate_to_bf16_grid(x - hi)
    return hi.astype(BF16), mid.astype(BF16), (x - hi - mid).astype(BF16)


def _position_tables(seq, tq, ub):
    slopes = jnp.array(ALIBI_SLOPES, F32)[:, None] * LOG2E
    pos = jnp.arange(seq, dtype=F32)[None, :]
    ones = jnp.ones((N_HEADS, seq), BF16)
    lane = jnp.arange(HEAD_W)[None, None, :]

    def lanes(cols):
        table = jnp.zeros((N_HEADS, seq, HEAD_W), BF16)
        for j, col in enumerate(cols):
            table = jnp.where(lane == j, col[:, :, None], table)
        return table

    kpos = lanes([*_split3(slopes * pos), ones, ones, ones])
    qpos = lanes([ones, ones, ones, *_split3(-slopes * pos - ub)])
    kk = jnp.arange(tq, dtype=jnp.int32)[:, None]
    qi = jnp.arange(tq, dtype=jnp.int32)[None, :]
    corr = slopes[:, :, None] * (-2.0 * jnp.maximum(kk - qi, 0).astype(F32))[None]
    corr = jnp.where((kk // CHUNK <= qi // CHUNK)[None], corr, NEG_INF)
    return kpos, qpos, corr


def _diff_prompt(q, k, vt, gate, lam_vecs, subln_g, lam_init, tq, heads, ub):
    bsz, seq, _ = q.shape
    tq = min(tq, seq)
    width = heads * HEAD_W
    grid = (bsz, N_HEADS // heads, seq // tq)
    q_spec = pl.BlockSpec((1, tq, width), lambda b, h, i: (b, i, h))
    once = dict(pipeline_mode=pl.Buffered(1))
    k_spec = pl.BlockSpec((1, seq, width), lambda b, h, i: (b, 0, h), **once)
    vt_spec = pl.BlockSpec((1, width, seq), lambda b, h, i: (b, h, 0), **once)
    tail_specs = [_const_spec((1, DK_B))] * 4 + [_const_spec((1, HEAD_W))]
    out_shape = jax.ShapeDtypeStruct((bsz, seq, MIX_W), BF16)
    acc_buf = pltpu.VMEM((heads, ACC_ROWS, 2 * tq), F32)
    sem = _params(("arbitrary", "arbitrary", "arbitrary"))

    def fixed():
        kpos, qpos, corr = _position_tables(seq, tq, ub)
        return pl.pallas_call(
            functools.partial(_diff_fixed_kernel, tq=tq, heads=heads, lam_init=lam_init),
            grid=grid,
            in_specs=[q_spec,
                      pl.BlockSpec((heads, tq, HEAD_W), lambda b, h, i: (h, i, 0)),
                      k_spec,
                      pl.BlockSpec((heads, seq, HEAD_W), lambda b, h, i: (h, 0, 0), **once),
                      vt_spec, q_spec,
                      pl.BlockSpec((heads, tq, tq), lambda b, h, i: (h, 0, 0), **once)] + tail_specs,
            out_specs=q_spec, out_shape=out_shape, scratch_shapes=[acc_buf],
            compiler_params=sem, name="diff_fixed",
        )(q, qpos, k, kpos, vt, gate, corr, *lam_vecs, subln_g)

    def exact():
        return pl.pallas_call(
            functools.partial(_diff_exact_kernel, tq=tq, heads=heads, lam_init=lam_init),
            grid=grid,
            in_specs=[q_spec, k_spec, vt_spec, q_spec,
                      pl.BlockSpec((heads, 2, tq, tq), lambda b, h, i: (h, 0, 0, 0), **once)] + tail_specs,
            out_specs=q_spec, out_shape=out_shape,
            scratch_shapes=[pltpu.VMEM((heads, 1, 2 * tq), F32), acc_buf],
            compiler_params=sem, name="diff_exact",
        )(q, k, vt, gate, _alibi_tables(tq), *lam_vecs, subln_g)

    return lax.cond(ub <= FIXED_SHIFT_MAX, fixed, exact)


def _diff_sample_kernel(q_ref, k_ref, v_ref, kp_ref, vp_ref, gate_ref, slope_ref, lq1_ref, lk1_ref, lq2_ref,
                        lk2_ref, sg_ref, o_ref, *, lam_init):
    rows = q_ref.shape[1]
    past = kp_ref.shape[1] // N_HEADS
    lam = _diff_lambda((lq1_ref, lk1_ref, lq2_ref, lk2_ref), lam_init)
    r = lax.broadcasted_iota(jnp.int32, (2 * rows, 1), 0)
    r = jnp.where(r >= rows, r - rows, r)
    dist_past = (past + r - lax.broadcasted_iota(jnp.int32, (2 * rows, past), 1)).astype(F32)
    dist_new = jnp.abs(r - lax.broadcasted_iota(jnp.int32, (2 * rows, rows), 1)).astype(F32)
    for h in range(N_HEADS):
        hs = slice(h * HEAD_W, (h + 1) * HEAD_W)
        slope = slope_ref[h][:, 0:1]
        qq = _split_maps(q_ref[0, :, hs])
        k_past = kp_ref[0, pl.ds(h, past, stride=N_HEADS), :].astype(BF16)
        v_past = vp_ref[0, pl.ds(h, past, stride=N_HEADS), :].astype(BF16)
        s_past = _dot_nt(qq, k_past) - slope * dist_past
        s_new = _dot_nt(qq, k_ref[0, :, hs]) - slope * dist_new
        m = jnp.maximum(jnp.max(s_past, axis=-1, keepdims=True), jnp.max(s_new, axis=-1, keepdims=True))
        acc = (_dot(jnp.exp2(s_past - m).astype(BF16), jnp.concatenate([v_past, _ones_col(past)], axis=-1))
               + _dot(jnp.exp2(s_new - m).astype(BF16), jnp.concatenate([v_ref[0, :, hs], _ones_col(rows)], axis=-1)))
        o = (acc[:rows, :HEAD_W] / acc[:rows, HEAD_W:HEAD_W + 1]
             - lam * (acc[rows:, :HEAD_W] / acc[rows:, HEAD_W:HEAD_W + 1]))
        o_ref[0, :, hs] = _diff_gate(o, lam_init, sg_ref[...], gate_ref[0, :, hs])


def _diff_sample(q, k_new, v_new, k_past, v_past, gate, slopes, lam_vecs, subln_g, lam_init):
    bsz, rows, _ = q.shape
    past = k_past.shape[1]
    tok = pl.BlockSpec((1, rows, MIX_W), lambda b: (b, 0, 0))
    cache = pl.BlockSpec((1, past * N_HEADS, HEAD_W), lambda b: (b, 0, 0))
    head_rows = lambda a: a.reshape(bsz, past * N_HEADS, HEAD_W)
    return pl.pallas_call(
        functools.partial(_diff_sample_kernel, lam_init=lam_init),
        grid=(bsz,),
        in_specs=[tok, tok, tok, cache, cache, tok, _const_spec((N_HEADS, 1, HEAD_W))]
                 + [_const_spec((1, DK_B))] * 4 + [_const_spec((1, HEAD_W))],
        out_specs=tok,
        out_shape=jax.ShapeDtypeStruct((bsz, rows, MIX_W), BF16),
        compiler_params=_params(("arbitrary",)),
        name="diff_sample",
    )(q, k_new, v_new, head_rows(k_past), head_rows(v_past), gate, slopes, *lam_vecs, subln_g)


def _cross_attn(xq, gate, mem_k, mem_v):
    outs = []
    for h in range(N_HEADS):
        sl = slice(h * HEAD_W, (h + 1) * HEAD_W)
        s = _dot_nt(xq[:, sl], mem_k[:, sl])
        p = jnp.exp2(s - jnp.max(s, axis=-1, keepdims=True))
        outs.append(_dot(p.astype(BF16), mem_v[:, sl]) / jnp.sum(p, axis=-1, keepdims=True))
    return (jnp.concatenate(outs, axis=-1) * gate).astype(BF16)


def _out_even_kernel(x_ref, oa_ref, ob_ref, xq_ref, gx_ref, mk_ref, mv_ref, w_ref, y_ref):
    ox = _cross_attn(xq_ref[0], gx_ref[0].astype(F32), mk_ref[0], mv_ref[0])
    mixed = jnp.concatenate([oa_ref[0], ob_ref[0], ox], axis=-1)
    y_ref[0] = x_ref[0] + _dot(mixed, w_ref[...])


def _out_even(x, oa, ob, xq, gx, mem_k, mem_v, w_out, tm):
    bsz, seq, _ = x.shape
    tm = min(tm, seq)
    tok = lambda width: pl.BlockSpec((1, tm, width), lambda b, i: (b, i, 0))
    mem_spec = pl.BlockSpec((1, N_MEM, MIX_W), lambda b, i: (b, 0, 0))
    return pl.pallas_call(
        _out_even_kernel,
        grid=(bsz, seq // tm),
        in_specs=[tok(D_MODEL), tok(MIX_W), tok(MIX_W), tok(MIX_W), tok(MIX_W), mem_spec, mem_spec,
                  _const_spec((3 * MIX_W, D_MODEL))],
        out_specs=tok(D_MODEL),
        out_shape=jax.ShapeDtypeStruct(x.shape, F32),
        compiler_params=_params(("arbitrary", "arbitrary")),
        name="out_even",
    )(x, oa, ob, xq, gx, mem_k, mem_v, w_out)


_O_CU, _O_CG, _O_CZ = 0, D_MODEL, 2 * D_MODEL
_O_XQ, _O_XZ = 3 * D_MODEL, 3 * D_MODEL + MIX_W


def _odd_kernel(x_ref, hist_ref, g_ref, w_in_ref, cw_ref, cb_ref, lg_ref, lb_ref, xqg_ref, mk_ref, mv_ref,
                w_out_ref, y_ref, tail_ref, u_sc, conv_sc, *, tm):
    i = pl.program_id(1)

    @pl.when(i == 0)
    def _():
        u_sc[0:CONV_HIST, :] = hist_ref[0]

    xn = _rms_rows(x_ref[0], g_ref[...]).astype(BF16)

    def proj(c0, width):
        return _dot(xn, w_in_ref[:, c0:c0 + width])

    u = proj(_O_CU, D_MODEL) * _sigmoid(proj(_O_CG, D_MODEL))
    u_sc[CONV_HIST:CONV_HIST + tm, :] = u
    rows = min(CONV_ROWS, tm)
    for c0 in range(0, D_MODEL, HEAD_W):
        cs = slice(c0, c0 + HEAD_W)
        for r0 in range(0, tm, rows):
            acc = jnp.broadcast_to(cb_ref[:, cs], (rows, HEAD_W))
            span = rows + CONV_HIST
            aligned = u_sc[r0:r0 + span, cs]
            for phase in range(SUBLANES):
                offs = [o for o in range(CONV_HIST - (CONV_W - 1), CONV_HIST + 1) if o % SUBLANES == phase]
                win = pltpu.roll(aligned, span - phase, axis=0) if phase else aligned
                for off in offs:
                    j = off - (CONV_HIST - (CONV_W - 1))
                    acc = acc + cw_ref[j:j + 1, cs] * win[off - phase:off - phase + rows]
            conv_sc[r0:r0 + rows, cs] = acc
    acc = conv_sc[...]
    mu = jnp.mean(acc, axis=-1, keepdims=True)
    cen = acc - mu
    var = jnp.mean(cen * cen, axis=-1, keepdims=True)
    c = _silu(cen * lax.rsqrt(var + EPS) * lg_ref[...] + lb_ref[...])
    c = (c * _silu(proj(_O_CZ, D_MODEL))).astype(BF16)
    xq = proj(_O_XQ, MIX_W)
    xqg = xqg_ref[...] * ((HEAD_W ** -0.5) * LOG2E)
    xq = jnp.concatenate([_rms_rows(xq[:, h * HEAD_W:(h + 1) * HEAD_W], xqg) for h in range(N_HEADS)], axis=-1)
    ox = _cross_attn(xq.astype(BF16), _silu(proj(_O_XZ, MIX_W)), mk_ref[0], mv_ref[0])
    y_ref[0] = x_ref[0] + _dot(c, w_out_ref[0:D_MODEL, :]) + _dot(ox, w_out_ref[D_MODEL:D_MODEL + MIX_W, :])
    tail = u_sc[tm:tm + CONV_HIST, :]
    u_sc[0:CONV_HIST, :] = tail
    tail_ref[0] = tail


def _odd_layer(x, hist, norm_g, w_in, conv_w, conv_b, ln_g, ln_b, xqg, mem_k, mem_v, w_out, tm):
    bsz, seq, _ = x.shape
    tm = min(tm, seq)
    tok = pl.BlockSpec((1, tm, D_MODEL), lambda b, i: (b, i, 0))
    hist_spec = pl.BlockSpec((1, CONV_HIST, D_MODEL), lambda b, i: (b, 0, 0))
    mem_spec = pl.BlockSpec((1, N_MEM, MIX_W), lambda b, i: (b, 0, 0))
    vec = _const_spec((1, D_MODEL))
    return pl.pallas_call(
        functools.partial(_odd_kernel, tm=tm),
        grid=(bsz, seq // tm),
        in_specs=[tok, hist_spec, vec, _const_spec(w_in.shape), _const_spec(conv_w.shape), vec, vec, vec,
                  _const_spec((1, HEAD_W)), mem_spec, mem_spec, _const_spec(w_out.shape)],
        out_specs=[tok, hist_spec],
        out_shape=[jax.ShapeDtypeStruct(x.shape, F32), jax.ShapeDtypeStruct((bsz, CONV_HIST, D_MODEL), F32)],
        scratch_shapes=[pltpu.VMEM((CONV_HIST + tm, D_MODEL), F32), pltpu.VMEM((tm, D_MODEL), F32)],
        compiler_params=_params(("arbitrary", "arbitrary")),
        name="odd_layer",
    )(x, hist, norm_g, w_in, conv_w, conv_b, ln_g, ln_b, xqg, mem_k, mem_v, w_out)


def _even_layer(x, mem_k, mem_v, hist, wts, lam_init, tm_in, tm_out, chunk, tq):
    (norm_g, w_main, w_gate, gate_bias, grp, qn, kn, xqg, mlstm_g, slopes, lam_vecs, subln_g, w_out) = wts
    bsz, seq, _ = x.shape
    (qa, ka, va, ga, gates, qb, kb, kbh, vb, vbh, gb, xq, gx) = _in_even(
        x.reshape(bsz * seq, D_MODEL), norm_g, w_main, w_gate, gate_bias, grp, qn, kn, xqg, tm_in,
        v_transposed_seq=seq if hist is None else None)
    tok = lambda a: a.reshape(bsz, seq, a.shape[-1])
    if hist is None:
        state0 = jnp.zeros((bsz, N_HEADS, ACC_ROWS, HEAD_W), F32)
        m0 = jnp.zeros((bsz, N_HEADS, 1, HEAD_W), F32)
    else:
        k_past, v_past, c0, n0, m0 = hist
        state0 = jnp.concatenate([jnp.swapaxes(c0, -1, -2), n0[..., None, :],
                                  jnp.zeros((bsz, N_HEADS, ACC_ROWS - HEAD_W - 1, HEAD_W), F32)], axis=-2)
        m0 = jnp.broadcast_to(m0[..., None, None], (bsz, N_HEADS, 1, HEAD_W))
    oa, state1, m1 = _mlstm(tok(qa), tok(ka), tok(va), tok(gates), tok(ga), mlstm_g, state0, m0, chunk)
    if hist is None:
        ub = (8.0 * LOG2E * 1.01) * jnp.max(jnp.abs(qn)) * jnp.max(jnp.abs(kn))
        ob = _diff_prompt(tok(qb), tok(kbh), vbh, tok(gb), lam_vecs, subln_g, lam_init, tq, heads=4, ub=ub)
    else:
        ob = _diff_sample(tok(qb), tok(kbh), tok(vbh), k_past, v_past, tok(gb), slopes, lam_vecs, subln_g,
                          lam_init)
    y = _out_even(x, oa, ob, tok(xq), tok(gx), mem_k, mem_v, w_out, tm_out)
    return (y, kb.reshape(bsz, seq, N_HEADS, HEAD_W), vb.reshape(bsz, seq, N_HEADS, HEAD_W),
            jnp.swapaxes(state1[..., :HEAD_W, :], -1, -2), state1[..., HEAD_W, :], m1[:, :, 0, 0])


def kernel(x_prompt, x_sample, mem_prompt, cache_xk, cache_xv, cache_k, cache_v, state_C, state_n, state_m,
           state_conv, norm_g, w_in_a, b_ig, b_fg, mlstm_norm_g, qn_g, kn_g, lam_q1, lam_k1, lam_q2, lam_k2,
           subln_g, w_out_a, w_in_c, conv_w, conv_b, conv_ln_g, conv_ln_b, w_out_c, mem_norm_g, w_mem_kv,
           xq_norm_g, xk_norm_g):
    depth = norm_g.shape[0]
    bsz = x_prompt.shape[0]
    dec = x_sample.shape[0]
    p_xk, p_xv = _mem_kv(mem_prompt, mem_norm_g, w_mem_kv, xk_norm_g)
    mem_k_p, mem_v_p = p_xk.astype(BF16), p_xv.astype(BF16)
    mem_k_s = cache_xk.reshape(depth, dec, N_MEM, MIX_W).astype(BF16)
    mem_v_s = cache_xv.reshape(depth, dec, N_MEM, MIX_W).astype(BF16)

    lane_grp = jnp.arange(MIX_W) // DK_B
    grp = (lane_grp[:, None] == lane_grp[None, :]).astype(BF16)
    slopes = jnp.broadcast_to(jnp.array(ALIBI_SLOPES, F32)[:, None, None] * LOG2E, (N_HEADS, 1, HEAD_W))
    n_gate = 2 * N_HEADS
    gate0 = 5 * MIX_W

    yp, ys = x_prompt, x_sample
    outs = {name: [] for name in ("p_k", "p_v", "p_C", "p_n", "p_m", "p_conv",
                                  "s_k", "s_v", "s_C", "s_n", "s_m", "s_conv")}
    for layer in range(depth):
        if layer % 2 == 0:
            e = layer // 2
            w = w_in_a[e].astype(BF16)
            w_main = jnp.concatenate([w[:, :gate0], w[:, gate0 + n_gate:]], axis=1)
            w_gate = jnp.pad(w[:, gate0:gate0 + n_gate], ((0, 0), (0, HEAD_W - n_gate)))
            gate_bias = jnp.pad(jnp.concatenate([b_ig[e], b_fg[e]]), (0, HEAD_W - n_gate)).reshape(1, HEAD_W)
            wts = (norm_g[layer].reshape(1, D_MODEL), w_main, w_gate, gate_bias, grp,
                   jnp.tile(qn_g[e], MIX_W // DK_B).reshape(1, MIX_W),
                   jnp.tile(kn_g[e], MIX_W // DK_B).reshape(1, MIX_W),
                   xq_norm_g[layer].reshape(1, HEAD_W), mlstm_norm_g[e].reshape(1, MIX_W), slopes,
                   tuple(v[e].reshape(1, DK_B) for v in (lam_q1, lam_k1, lam_q2, lam_k2)),
                   subln_g[e].reshape(1, HEAD_W), w_out_a[e].astype(BF16))
            lam_init = _lambda_init(layer)
            yp, k_new, v_new, c1, n1, m1 = _even_layer(yp, mem_k_p[layer], mem_v_p[layer], None, wts, lam_init,
                                                       tm_in=256, tm_out=512, chunk=256, tq=512)
            for name, val in zip(("p_k", "p_v", "p_C", "p_n", "p_m"), (k_new, v_new, c1, n1, m1)):
                outs[name].append(val)
            hist = (cache_k[e], cache_v[e], state_C[e], state_n[e], state_m[e])
            ys, k_new, v_new, c1, n1, m1 = _even_layer(ys, mem_k_s[layer], mem_v_s[layer], hist, wts, lam_init,
                                                       tm_in=256, tm_out=512, chunk=256, tq=512)
            for name, val in zip(("s_k", "s_v", "s_C", "s_n", "s_m"), (k_new, v_new, c1, n1, m1)):
                outs[name].append(val)
        else:
            o = layer // 2
            vec = lambda a: a.reshape(1, -1)
            wts = (vec(norm_g[layer]), w_in_c[o].astype(BF16), jnp.pad(conv_w[o], ((0, 1), (0, 0))),
                   vec(conv_b[o]), vec(conv_ln_g[o]), vec(conv_ln_b[o]), vec(xq_norm_g[layer]))
            w_out = w_out_c[o].astype(BF16)
            pad = CONV_HIST - (CONV_W - 1)
            zero_hist = jnp.zeros((bsz, CONV_HIST, D_MODEL), F32)
            yp, tail = _odd_layer(yp, zero_hist, *wts, mem_k_p[layer], mem_v_p[layer], w_out, tm=512)
            outs["p_conv"].append(tail[:, pad:])
            hist = jnp.pad(state_conv[o], ((0, 0), (pad, 0), (0, 0)))
            ys, tail = _odd_layer(ys, hist, *wts, mem_k_s[layer], mem_v_s[layer], w_out, tm=256)
            outs["s_conv"].append(tail[:, pad:])

    head5 = lambda a: a.reshape(a.shape[:-1] + (N_HEADS, HEAD_W))
    st = {name: jnp.stack(vals) for name, vals in outs.items()}
    return (yp, ys, head5(p_xk), head5(p_xv), st["p_k"], st["p_v"], st["p_C"], st["p_n"], st["p_m"], st["p_conv"],
            st["s_k"], st["s_v"], st["s_C"], st["s_n"], st["s_m"], st["s_conv"])
```
